```python
import jax, jax.numpy as jnp
from jax import lax
import numpy as np

D_MODEL = 4096
BATCH = 2
SEQ = 8192
DEPTH = 2

N_MEM = 256
E_A = D_MODEL // 2
CONV_A = 3
E_B = D_MODEL // 2
POOL_WINDOWS = (2, 4, 8, 16)
N_POOL_GROUPS = len(POOL_WINDOWS)
G_B = E_B // N_POOL_GROUPS
GO_B = D_MODEL // N_POOL_GROUPS
E_C = D_MODEL // 2
CONV_C = 31
X_HEADS = 4
X_HEAD_DIM = D_MODEL // 16
X_W = X_HEADS * X_HEAD_DIM
N_BRANCH = 4
EPS = 1e-6

IN_SIZES = (E_A, E_A, E_A, E_A,
            E_B, E_B,
            E_C, E_C, E_C,
            X_W,
            N_BRANCH * D_MODEL)
N_IN = sum(IN_SIZES)
IN_SPLITS = tuple(int(s) for s in np.cumsum(IN_SIZES)[:-1])

kernel_name = "gated_parallel_conv_pool_conformer_memxattn"


def rmsnorm(x, g):
    xf = x.astype(jnp.float32)
    y = xf * lax.rsqrt(jnp.mean(xf * xf, axis=-1, keepdims=True) + EPS)
    return (y * g.astype(jnp.float32)).astype(x.dtype)


def layernorm(x, g, b):
    xf = x.astype(jnp.float32)
    mu = jnp.mean(xf, axis=-1, keepdims=True)
    var = jnp.mean(jnp.square(xf - mu), axis=-1, keepdims=True)
    y = (xf - mu) * lax.rsqrt(var + EPS)
    return (y * g.astype(jnp.float32) + b.astype(jnp.float32)).astype(x.dtype)


def causal_depthwise_conv(x, w):
    k, c = w.shape
    return lax.conv_general_dilated(
        x, w.astype(x.dtype)[:, None, :], window_strides=(1,), padding=[(k - 1, 0)],
        dimension_numbers=("NWC", "WIO", "NWC"), feature_group_count=c)


def causal_multiscale_pool(u):
    t = u.shape[1]
    cs = jnp.cumsum(u.astype(jnp.float32), axis=1)
    pos = jnp.arange(t, dtype=jnp.float32)
    outs = []
    for gi, w in enumerate(POOL_WINDOWS):
        c = cs[:, :, gi]
        c_shift = jnp.pad(c, ((0, 0), (w, 0), (0, 0)))[:, :t]
        cnt = jnp.minimum(pos + 1.0, float(w))[None, :, None]
        outs.append((c - c_shift) / cnt)
    mean = jnp.stack(outs, axis=2)
    return mean.astype(u.dtype) - u


def setup_inputs(seed: int = 0) -> dict:
    key = jax.random.key(seed)
    ks = jax.random.split(key, 20)
    f32 = jnp.float32
    nrm = lambda k, shape, fan_in: jax.random.normal(k, shape, f32) * (fan_in ** -0.5)
    gain = lambda k, shape: 1.0 + 0.05 * jax.random.normal(k, shape, f32)
    return {
        "x": jax.random.normal(ks[0], (BATCH, SEQ, D_MODEL), f32),
        "mem": jax.random.normal(ks[1], (BATCH, N_MEM, D_MODEL), f32),
        "g_pre": gain(ks[2], (DEPTH, D_MODEL)),
        "g_post": gain(ks[3], (DEPTH, D_MODEL)),
        "g_mem": gain(ks[4], (DEPTH, D_MODEL)),
        "w_in": nrm(ks[5], (DEPTH, D_MODEL, N_IN), D_MODEL),
        "conv_a_w": nrm(ks[6], (DEPTH, CONV_A, E_A), CONV_A),
        "w_out_a": nrm(ks[7], (DEPTH, E_A, D_MODEL), E_A),
        "pool_scale": gain(ks[8], (DEPTH, E_B)),
        "w_pool": nrm(ks[9], (DEPTH, N_POOL_GROUPS, G_B, GO_B), G_B),
        "conv_c_w": nrm(ks[10], (DEPTH, CONV_C, E_C), CONV_C),
        "conv_c_b": 0.02 * jax.random.normal(ks[11], (DEPTH, E_C), f32),
        "ln_c_g": gain(ks[12], (DEPTH, E_C)),
        "ln_c_b": 0.02 * jax.random.normal(ks[13], (DEPTH, E_C), f32),
        "w_out_c": nrm(ks[14], (DEPTH, E_C, D_MODEL), E_C),
        "w_mem_kv": nrm(ks[15], (DEPTH, D_MODEL, 2 * X_W), D_MODEL),
        "w_out_x": nrm(ks[16], (DEPTH, X_W, D_MODEL), X_W),
        "w_o": nrm(ks[17], (DEPTH, D_MODEL, D_MODEL), D_MODEL),
    }


def reference(x, mem, g_pre, g_post, g_mem, w_in, conv_a_w, w_out_a, pool_scale, w_pool,
              conv_c_w, conv_c_b, ln_c_g, ln_c_b, w_out_c, w_mem_kv, w_out_x, w_o):
    bsz, t, _ = x.shape
    for l in range(DEPTH):
        h = rmsnorm(x, g_pre[l])
        proj = jnp.einsum("btd,dn->btn", h, w_in[l])
        (v_a, b_a, c_a, z_a, u_b, z_b, a_c, gt_c, z_c, q_x, gate_logits) = jnp.split(
            proj, IN_SPLITS, axis=-1)

        y_a = b_a * causal_depthwise_conv(c_a * v_a, conv_a_w[l])
        br_a = jnp.einsum("bte,ed->btd", y_a * jax.nn.silu(z_a), w_out_a[l])

        p = causal_multiscale_pool(u_b.reshape(bsz, t, N_POOL_GROUPS, G_B))
        p = p * pool_scale[l].reshape(N_POOL_GROUPS, G_B) * jax.nn.silu(
            z_b.reshape(bsz, t, N_POOL_GROUPS, G_B))
        br_b = jnp.einsum("btgc,gco->btgo", p, w_pool[l]).reshape(bsz, t, D_MODEL)

        g_c = a_c * jax.nn.sigmoid(gt_c)
        d_c = causal_depthwise_conv(g_c, conv_c_w[l]) + conv_c_b[l]
        s_c = jax.nn.silu(layernorm(d_c, ln_c_g[l], ln_c_b[l])) * jax.nn.silu(z_c)
        br_c = jnp.einsum("bte,ed->btd", s_c, w_out_c[l])

        m = rmsnorm(mem, g_mem[l])
        kv = jnp.einsum("bmd,dn->bmn", m, w_mem_kv[l])
        k_m, v_m = jnp.split(kv, 2, axis=-1)
        qh = q_x.reshape(bsz, t, X_HEADS, X_HEAD_DIM)
        kh = k_m.reshape(bsz, N_MEM, X_HEADS, X_HEAD_DIM)
        vh = v_m.reshape(bsz, N_MEM, X_HEADS, X_HEAD_DIM)
        scores = jnp.einsum("bthd,bmhd->bhtm", qh.astype(jnp.float32), kh.astype(jnp.float32))
        probs = jax.nn.softmax(scores * (X_HEAD_DIM ** -0.5), axis=-1).astype(vh.dtype)
        att = jnp.einsum("bhtm,bmhd->bthd", probs, vh).reshape(bsz, t, X_W)
        br_x = jnp.einsum("btw,wd->btd", att, w_out_x[l])

        gates = jax.nn.sigmoid(gate_logits.reshape(bsz, t, N_BRANCH, D_MODEL))
        merged = (gates[:, :, 0] * br_a + gates[:, :, 1] * br_b
                  + gates[:, :, 2] * br_c + gates[:, :, 3] * br_x)
        y = jnp.einsum("btd,de->bte", merged, w_o[l])

        x = x + rmsnorm(y, g_post[l])
    return x
```

```python
import functools

import jax
import jax.numpy as jnp
from jax import lax
from jax.experimental import pallas as pl
from jax.experimental.pallas import tpu as pltpu

EPS = 1e-6
POOL_WINDOWS = (2, 4, 8, 16)
X_HEADS = 4

SUBLANES = 8
CONV_HALO = 32
POOL_HALO = 8
V7X_VMEM_BYTES = 64 * 1024 * 1024
VMEM_CAP_BYTES = V7X_VMEM_BYTES - 6 * 1024 * 1024

CONV_ROW_TILE = 512
MATMUL_ROW_TILE = 1024

F32 = jnp.float32
BF16 = jnp.bfloat16


def _dot(a, b):
    return jnp.dot(a, b, preferred_element_type=F32)


def _silu(z):
    return z * jax.nn.sigmoid(z)


def _row_tile(n_rows, preferred):
    t = min(n_rows, preferred)
    while n_rows % t or t % SUBLANES:
        t -= 1
    return t


def _params(n_grid_dims, est_bytes):
    limit = int(min(max(est_bytes, 32 * 1024 * 1024), VMEM_CAP_BYTES))
    return pltpu.CompilerParams(
        dimension_semantics=("arbitrary",) * n_grid_dims, vmem_limit_bytes=limit)


def _rmsnorm_body(x_ref, g_ref, o_ref):
    x = x_ref[...]
    ms = jnp.mean(x * x, axis=-1, keepdims=True)
    o_ref[...] = (x * lax.rsqrt(ms + EPS) * g_ref[...]).astype(o_ref.dtype)


def _rmsnorm(x, g, out_dtype):
    t, d = x.shape
    tm = _row_tile(t, 512)
    return pl.pallas_call(
        _rmsnorm_body,
        grid=(t // tm,),
        in_specs=[pl.BlockSpec((tm, d), lambda i: (i, 0)), pl.BlockSpec((1, d), lambda i: (0, 0))],
        out_specs=pl.BlockSpec((tm, d), lambda i: (i, 0)),
        out_shape=jax.ShapeDtypeStruct((t, d), out_dtype),
        compiler_params=_params(1, 6 * tm * d * 4),
        name="rmsnorm",
    )(x, g.reshape(1, d))


def _residual_body(x_ref, y_ref, g_ref, *rest):
    y = y_ref[...]
    ms = jnp.mean(y * y, axis=-1, keepdims=True)
    xn = x_ref[...] + y * lax.rsqrt(ms + EPS) * g_ref[...]
    if len(rest) == 1:
        rest[0][...] = xn
        return
    gn_ref, o_ref, h_ref = rest
    o_ref[...] = xn
    ms2 = jnp.mean(xn * xn, axis=-1, keepdims=True)
    h_ref[...] = (xn * lax.rsqrt(ms2 + EPS) * gn_ref[...]).astype(h_ref.dtype)


def _residual(x, y, g_post, g_next):
    t, d = x.shape
    tm = _row_tile(t, 256)
    row = pl.BlockSpec((tm, d), lambda i: (i, 0))
    vec = pl.BlockSpec((1, d), lambda i: (0, 0))
    x_out = jax.ShapeDtypeStruct((t, d), F32)
    if g_next is None:
        args, in_specs, out_specs, out_shape = (), [row, row, vec], row, x_out
    else:
        args, in_specs = (g_next.reshape(1, d),), [row, row, vec, vec]
        out_specs, out_shape = [row, row], [x_out, jax.ShapeDtypeStruct((t, d), BF16)]
    out = pl.pallas_call(
        _residual_body,
        grid=(t // tm,),
        in_specs=in_specs,
        out_specs=out_specs,
        out_shape=out_shape,
        compiler_params=_params(1, 10 * tm * d * 4),
        name="residual_norm",
    )(x, y, g_post.reshape(1, d), *args)
    return (out, None) if g_next is None else out


def _ln_gate_body(d_ref, z_ref, g_ref, b_ref, o_ref):
    d = d_ref[...]
    mu = jnp.mean(d, axis=-1, keepdims=True)
    c = d - mu
    var = jnp.mean(c * c, axis=-1, keepdims=True)
    y = c * lax.rsqrt(var + EPS) * g_ref[...] + b_ref[...]
    o_ref[...] = (_silu(y) * z_ref[...].astype(F32)).astype(o_ref.dtype)


def _ln_gate(dc, zc, g, b):
    t, e = dc.shape
    tm = _row_tile(t, 512)
    row = pl.BlockSpec((tm, e), lambda i: (i, 0))
    vec = pl.BlockSpec((1, e), lambda i: (0, 0))
    return pl.pallas_call(
        _ln_gate_body,
        grid=(t // tm,),
        in_specs=[row, row, vec, vec],
        out_specs=row,
        out_shape=jax.ShapeDtypeStruct((t, e), BF16),
        compiler_params=_params(1, 8 * tm * e * 4),
        name="ln_gate",
    )(dc, zc, g.reshape(1, e), b.reshape(1, e))


def _proj_specs(d, tm, cb, col_starts):
    h_spec = pl.BlockSpec((tm, d), lambda c, i: (i, 0))
    w_specs = [pl.BlockSpec((d, cb), lambda c, i, s=s // cb: (0, s + c)) for s in col_starts]
    return h_spec, w_specs


def _conv_a_body(h_ref, wv_ref, wb_ref, wc_ref, wz_ref, cw_ref, o_ref, ext_ref, *, tiles_per_seq):
    i = pl.program_id(1)
    tm, cb = o_ref.shape
    kw = cw_ref.shape[0]

    @pl.when(i % tiles_per_seq == 0)
    def _():
        ext_ref[0:CONV_HALO, :] = jnp.zeros((CONV_HALO, cb), F32)

    h = h_ref[...]
    v = _dot(h, wv_ref[...])
    c = _dot(h, wc_ref[...])
    ext_ref[CONV_HALO:CONV_HALO + tm, :] = c * v
    b = _dot(h, wb_ref[...])
    z = _dot(h, wz_ref[...])
    conv = cw_ref[kw - 1:kw, :] * ext_ref[CONV_HALO:CONV_HALO + tm, :]
    for k in range(kw - 1):
        lo = CONV_HALO - (kw - 1 - k)
        conv = conv + cw_ref[k:k + 1, :] * ext_ref[lo:lo + tm, :]
    o_ref[...] = (b * conv * _silu(z)).astype(o_ref.dtype)
    ext_ref[0:CONV_HALO, :] = ext_ref[tm:tm + CONV_HALO, :]


def _pool_body(h_ref, wu_ref, wz_ref, ps_ref, o_ref, e1_ref, e2_ref, e4_ref, e8_ref, *, tiles_per_seq):
    g = pl.program_id(0)
    i = pl.program_id(1)
    tm, cb = o_ref.shape
    exts = (e1_ref, e2_ref, e4_ref, e8_ref)

    @pl.when(i % tiles_per_seq == 0)
    def _():
        for e in exts:
            e[0:POOL_HALO, :] = jnp.zeros((POOL_HALO, cb), F32)

    h = h_ref[...]
    u = _dot(h, wu_ref[...])
    z = _dot(h, wz_ref[...])
    sums = []
    s = u
    for e, w in zip(exts, (1, 2, 4, 8)):
        e[POOL_HALO:POOL_HALO + tm, :] = s
        s = s + e[POOL_HALO - w:POOL_HALO - w + tm, :]
        sums.append(s)
    for e in exts:
        e[0:POOL_HALO, :] = e[tm:tm + POOL_HALO, :]
    wsum = jnp.where(g == 0, sums[0], jnp.where(g == 1, sums[1], jnp.where(g == 2, sums[2], sums[3])))
    win = jnp.where(g == 0, POOL_WINDOWS[0],
                    jnp.where(g == 1, POOL_WINDOWS[1],
                              jnp.where(g == 2, POOL_WINDOWS[2], POOL_WINDOWS[3]))).astype(F32)
    pos = ((i % tiles_per_seq) * tm + lax.broadcasted_iota(jnp.int32, (tm, 1), 0)).astype(F32)
    cnt = jnp.minimum(pos + 1.0, win)
    o_ref[...] = ((wsum / cnt - u) * ps_ref[...] * _silu(z)).astype(o_ref.dtype)


def _conf_body(h_ref, wa_ref, wg_ref, wz_ref, cw_ref, cbias_ref, d_ref, zc_ref, ext_ref, wrow_ref,
               *, tiles_per_seq, rows_per_chunk):
    i = pl.program_id(1)
    tm, cb = d_ref.shape
    kw = cw_ref.shape[0]
    n_sub = rows_per_chunk // SUBLANES

    @pl.when(i == 0)
    def _():
        for k in range(kw):
            wrow_ref[k] = jnp.broadcast_to(cw_ref[k:k + 1, :], (SUBLANES, cb))
        wrow_ref[kw] = jnp.broadcast_to(cbias_ref[...], (SUBLANES, cb))

    @pl.when(i % tiles_per_seq == 0)
    def _():
        for s in range(SUBLANES):
            ext_ref[s] = jnp.zeros(ext_ref.shape[1:], F32)

    h = h_ref[...]
    z = _dot(h, wz_ref[...])
    zc_ref[...] = _silu(z).astype(zc_ref.dtype)
    a = _dot(h, wa_ref[...])
    gt = _dot(h, wg_ref[...])
    glu = a * jax.nn.sigmoid(gt)
    for s in range(SUBLANES):
        ext_ref[s, CONV_HALO + s:CONV_HALO + s + tm, :] = glu

    def chunk(r, carry):
        base = pl.multiple_of(r * rows_per_chunk, rows_per_chunk)
        accs = [wrow_ref[kw]] * n_sub
        for k in range(kw):
            a8, s = divmod(kw - 1 - k, SUBLANES)
            wk = wrow_ref[k]
            for j in range(n_sub):
                row = base + (CONV_HALO + SUBLANES * (j - a8))
                accs[j] = accs[j] + wk * ext_ref[s, pl.ds(row, SUBLANES), :]
        for j in range(n_sub):
            d_ref[pl.ds(base + SUBLANES * j, SUBLANES), :] = accs[j]
        return carry

    lax.fori_loop(0, tm // rows_per_chunk, chunk, 0)
    keep = CONV_HALO + SUBLANES
    for s in range(SUBLANES):
        ext_ref[s, 0:keep, :] = ext_ref[s, tm:tm + keep, :]


def _attn_body(h_ref, wq_ref, k_ref, v_ref, o_ref):
    q = _dot(h_ref[...], wq_ref[...])
    dh = q.shape[1] // X_HEADS
    scale = dh ** -0.5
    for hd in range(X_HEADS):
        cols = slice(hd * dh, (hd + 1) * dh)
        qh = q[:, cols].astype(BF16)
        s = lax.dot_general(qh, k_ref[0, :, cols], (((1,), (1,)), ((), ())),
                            preferred_element_type=F32) * scale
        p = jnp.exp(s - jnp.max(s, axis=-1, keepdims=True))
        p = p / jnp.sum(p, axis=-1, keepdims=True)
        o_ref[:, cols] = _dot(p.astype(BF16), v_ref[0, :, cols]).astype(o_ref.dtype)


def _gates_body(h_ref, w_ref, o_ref):
    o_ref[...] = jax.nn.sigmoid(_dot(h_ref[...], w_ref[...])).astype(o_ref.dtype)


def _kv_body(m_ref, g_ref, w_ref, o_ref):
    m = m_ref[0]
    ms = jnp.mean(m * m, axis=-1, keepdims=True)
    mn = (m * lax.rsqrt(ms + EPS) * g_ref[...]).astype(BF16)
    o_ref[0] = _dot(mn, w_ref[...]).astype(o_ref.dtype)


def _merge_body(ya_ref, pb_ref, sc_ref, att_ref, woa_ref, wp_ref, woc_ref, wox_ref,
                g0_ref, g1_ref, g2_ref, g3_ref, o_ref):
    m = g0_ref[...].astype(F32) * _dot(ya_ref[...], woa_ref[...])
    m = m + g1_ref[...].astype(F32) * _dot(pb_ref[...], wp_ref[0])
    m = m + g2_ref[...].astype(F32) * _dot(sc_ref[...], woc_ref[...])
    m = m + g3_ref[...].astype(F32) * _dot(att_ref[...], wox_ref[...])
    o_ref[...] = m.astype(o_ref.dtype)


def _matmul_body(a_ref, w_ref, o_ref):
    o_ref[...] = _dot(a_ref[...], w_ref[...]).astype(o_ref.dtype)


def _layer(h, x, kv, seq_len, w_in, conv_a_w, w_out_a, pool_scale, w_pool, conv_c_w, conv_c_b,
           ln_c_g, ln_c_b, w_out_c, w_out_x, w_o, g_post, g_next):
    t, d = h.shape
    e_a = conv_a_w.shape[1]
    e_b = pool_scale.shape[0]
    e_c = conv_c_w.shape[1]
    x_w = w_out_x.shape[0]
    n_groups, g_b, go_b = w_pool.shape
    assert n_groups == len(POOL_WINDOWS) and n_groups * go_b == d
    off_va, off_ba, off_ca, off_za = 0, e_a, 2 * e_a, 3 * e_a
    off_ub = 4 * e_a
    off_zb = off_ub + e_b
    off_ac = off_zb + e_b
    off_gc = off_ac + e_c
    off_zc = off_gc + e_c
    off_q = off_zc + e_c
    off_g = off_q + x_w
    assert off_g + 4 * d == w_in.shape[1]

    tm_conv = _row_tile(seq_len, CONV_ROW_TILE)
    tps_conv = seq_len // tm_conv
    cb_a = min(e_a, 512)
    cb_c = min(e_c, 512)

    h_spec, w_specs = _proj_specs(d, tm_conv, cb_a, (off_va, off_ba, off_ca, off_za))
    ka = conv_a_w.shape[0]
    ya = pl.pallas_call(
        functools.partial(_conv_a_body, tiles_per_seq=tps_conv),
        grid=(e_a // cb_a, t // tm_conv),
        in_specs=[h_spec] + w_specs + [pl.BlockSpec((ka, cb_a), lambda c, i: (0, c))],
        out_specs=pl.BlockSpec((tm_conv, cb_a), lambda c, i: (i, c)),
        out_shape=jax.ShapeDtypeStruct((t, e_a), BF16),
        scratch_shapes=[pltpu.VMEM((CONV_HALO + tm_conv, cb_a), F32)],
        compiler_params=_params(2, 2 * (tm_conv * d * 2 + 4 * d * cb_a * 2) + 12 * tm_conv * cb_a * 4),
        name="branch_a",
    )(h, w_in, w_in, w_in, w_in, conv_a_w)

    h_spec, w_specs = _proj_specs(d, tm_conv, g_b, (off_ub, off_zb))
    pb = pl.pallas_call(
        functools.partial(_pool_body, tiles_per_seq=tps_conv),
        grid=(n_groups, t // tm_conv),
        in_specs=[h_spec] + w_specs + [pl.BlockSpec((1, g_b), lambda c, i: (0, c))],
        out_specs=pl.BlockSpec((tm_conv, g_b), lambda c, i: (i, c)),
        out_shape=jax.ShapeDtypeStruct((t, e_b), BF16),
        scratch_shapes=[pltpu.VMEM((POOL_HALO + tm_conv, g_b), F32)] * 4,
        compiler_params=_params(2, 2 * (tm_conv * d * 2 + 2 * d * g_b * 2) + 16 * tm_conv * g_b * 4),
        name="branch_b",
    )(h, w_in, w_in, pool_scale.reshape(1, e_b))

    h_spec, w_specs = _proj_specs(d, tm_conv, cb_c, (off_ac, off_gc, off_zc))
    kc = conv_c_w.shape[0]
    assert (kc - 1) // SUBLANES * SUBLANES <= CONV_HALO and off_q % x_w == 0 and off_g % go_b == 0
    dc, zc = pl.pallas_call(
        functools.partial(_conf_body, tiles_per_seq=tps_conv, rows_per_chunk=min(32, tm_conv)),
        grid=(e_c // cb_c, t // tm_conv),
        in_specs=[h_spec] + w_specs + [pl.BlockSpec((kc, cb_c), lambda c, i: (0, c)),
                                       pl.BlockSpec((1, cb_c), lambda c, i: (0, c))],
        out_specs=[pl.BlockSpec((tm_conv, cb_c), lambda c, i: (i, c))] * 2,
        out_shape=[jax.ShapeDtypeStruct((t, e_c), F32), jax.ShapeDtypeStruct((t, e_c), BF16)],
        scratch_shapes=[pltpu.VMEM((SUBLANES, CONV_HALO + tm_conv + SUBLANES, cb_c), F32),
                        pltpu.VMEM((kc + 1, SUBLANES, cb_c), F32)],
        compiler_params=_params(2, 2 * (tm_conv * d * 2 + 3 * d * cb_c * 2) + 22 * tm_conv * cb_c * 4),
        name="branch_c",
    )(h, w_in, w_in, w_in, conv_c_w, conv_c_b.reshape(1, e_c))
    sc = _ln_gate(dc, zc, ln_c_g, ln_c_b)

    tm_x = _row_tile(seq_len, CONV_ROW_TILE)
    tps_x = seq_len // tm_x
    n_mem = kv.shape[1]
    att = pl.pallas_call(
        _attn_body,
        grid=(t // tm_x,),
        in_specs=[pl.BlockSpec((tm_x, d), lambda i: (i, 0)),
                  pl.BlockSpec((d, x_w), lambda i: (0, off_q // x_w)),
                  pl.BlockSpec((1, n_mem, x_w), lambda i: (i // tps_x, 0, 0)),
                  pl.BlockSpec((1, n_mem, x_w), lambda i: (i // tps_x, 0, 1))],
        out_specs=pl.BlockSpec((tm_x, x_w), lambda i: (i, 0)),
        out_shape=jax.ShapeDtypeStruct((t, x_w), BF16),
        compiler_params=_params(1, 2 * (tm_x * d * 2 + d * x_w * 2) + 8 * tm_x * x_w * 4),
        name="mem_attention",
    )(h, w_in, kv, kv)

    tm_g = _row_tile(t, MATMUL_ROW_TILE)
    nb = go_b
    gates = pl.pallas_call(
        _gates_body,
        grid=(4 * d // nb, t // tm_g),
        in_specs=[pl.BlockSpec((tm_g, d), lambda c, i: (i, 0)),
                  pl.BlockSpec((d, nb), lambda c, i: (0, off_g // nb + c))],
        out_specs=pl.BlockSpec((tm_g, nb), lambda c, i: (i, c)),
        out_shape=jax.ShapeDtypeStruct((t, 4 * d), BF16),
        compiler_params=_params(2, 2 * (tm_g * d * 2 + d * nb * 2 + tm_g * nb * 2) + 3 * tm_g * nb * 4),
        name="merge_gates",
    )(h, w_in)

    tm_m = _row_tile(t, CONV_ROW_TILE)
    n_nb = d // nb
    row = lambda width: pl.BlockSpec((tm_m, width), lambda n, i: (i, 0))
    gate = lambda k: pl.BlockSpec((tm_m, nb), lambda n, i: (i, k * n_nb + n))
    merged = pl.pallas_call(
        _merge_body,
        grid=(n_nb, t // tm_m),
        in_specs=[row(e_a), pl.BlockSpec((tm_m, g_b), lambda n, i: (i, n)), row(e_c), row(x_w),
                  pl.BlockSpec((e_a, nb), lambda n, i: (0, n)),
                  pl.BlockSpec((1, g_b, go_b), lambda n, i: (n, 0, 0)),
                  pl.BlockSpec((e_c, nb), lambda n, i: (0, n)),
                  pl.BlockSpec((x_w, nb), lambda n, i: (0, n)),
                  gate(0), gate(1), gate(2), gate(3)],
        out_specs=pl.BlockSpec((tm_m, nb), lambda n, i: (i, n)),
        out_shape=jax.ShapeDtypeStruct((t, d), BF16),
        compiler_params=_params(
            2, 2 * (tm_m * (e_a + g_b + e_c + x_w) * 2 + (e_a + g_b + e_c + x_w) * nb * 2
                    + 5 * tm_m * nb * 2) + 6 * tm_m * nb * 4),
        name="gated_merge",
    )(ya, pb, sc, att, w_out_a, w_pool, w_out_c, w_out_x, gates, gates, gates, gates)

    tm_o = _row_tile(t, MATMUL_ROW_TILE)
    y = pl.pallas_call(
        _matmul_body,
        grid=(d // nb, t // tm_o),
        in_specs=[pl.BlockSpec((tm_o, d), lambda n, i: (i, 0)),
                  pl.BlockSpec((d, nb), lambda n, i: (0, n))],
        out_specs=pl.BlockSpec((tm_o, nb), lambda n, i: (i, n)),
        out_shape=jax.ShapeDtypeStruct((t, d), F32),
        compiler_params=_params(2, 2 * (tm_o * d * 2 + d * nb * 2 + tm_o * nb * 4) + 2 * tm_o * nb * 4),
        name="out_proj",
    )(merged, w_o)
    return _residual(x, y, g_post, g_next)


def _memory_kv(mem, g_mem, w_mem_kv):
    b, n_mem, d = mem.shape
    n_out = w_mem_kv.shape[1]
    nb = min(n_out, 512)
    return pl.pallas_call(
        _kv_body,
        grid=(n_out // nb, b),
        in_specs=[pl.BlockSpec((1, n_mem, d), lambda n, bi: (bi, 0, 0)),
                  pl.BlockSpec((1, d), lambda n, bi: (0, 0)),
                  pl.BlockSpec((d, nb), lambda n, bi: (0, n))],
        out_specs=pl.BlockSpec((1, n_mem, nb), lambda n, bi: (bi, 0, n)),
        out_shape=jax.ShapeDtypeStruct((b, n_mem, n_out), BF16),
        compiler_params=_params(2, 2 * (n_mem * d * 4 + d * nb * 2) + 4 * n_mem * d * 4),
        name="memory_kv",
    )(mem, g_mem.reshape(1, d), w_mem_kv)


def kernel(x, mem, g_pre, g_post, g_mem, w_in, conv_a_w, w_out_a, pool_scale, w_pool, conv_c_w,
           conv_c_b, ln_c_g, ln_c_b, w_out_c, w_mem_kv, w_out_x, w_o):
    bsz, seq_len, d = x.shape
    depth = w_in.shape[0]
    xf = x.reshape(bsz * seq_len, d)
    h = _rmsnorm(xf, g_pre[0], BF16)
    for l in range(depth):
        kv = _memory_kv(mem, g_mem[l], w_mem_kv[l].astype(BF16))
        g_next = g_pre[l + 1] if l + 1 < depth else None
        xf, h = _layer(
            h, xf, kv, seq_len, w_in[l].astype(BF16), conv_a_w[l], w_out_a[l].astype(BF16),
            pool_scale[l], w_pool[l].astype(BF16), conv_c_w[l], conv_c_b[l], ln_c_g[l], ln_c_b[l],
            w_out_c[l].astype(BF16), w_out_x[l].astype(BF16), w_o[l].astype(BF16), g_post[l], g_next)
    return xf.reshape(bsz, seq_len, d)
```

```python
import functools
from typing import Callable, NamedTuple

import jax
import jax.numpy as jnp
from jax import lax
from jax.experimental import pallas as pl
from jax.experimental.pallas import tpu as pltpu

EPS = 1e-6
POOL_WINDOWS = (2, 4, 8, 16)
X_HEADS = 4

SUBLANES = 8
BF16_ROWS = 16
CONV_HALO = 32
POOL_HALO = 8
V7X_VMEM_BYTES = 64 * 1024 * 1024
VMEM_CAP_BYTES = V7X_VMEM_BYTES - 6 * 1024 * 1024

CONV_ROW_TILE = 512
SUB_ROWS_A = 256
SUB_ROWS_B = 128
SUB_ROWS_C = 128
MATMUL_ROW_TILE = 1024

F32 = jnp.float32
BF16 = jnp.bfloat16


def _dot(a, b):
    return jnp.dot(a, b, preferred_element_type=F32)


def _silu(z):
    return z * jax.nn.sigmoid(z)


def _row_tile(n_rows, preferred):
    t = min(n_rows, preferred)
    while n_rows % t or t % SUBLANES:
        t -= 1
    return t


def _params(n_grid_dims, est_bytes):
    limit = int(min(max(est_bytes, 32 * 1024 * 1024), VMEM_CAP_BYTES))
    return pltpu.CompilerParams(
        dimension_semantics=("arbitrary",) * n_grid_dims, vmem_limit_bytes=limit)


def _rmsnorm_body(x_ref, g_ref, o_ref):
    x = x_ref[...]
    ms = jnp.mean(x * x, axis=-1, keepdims=True)
    o_ref[...] = (x * lax.rsqrt(ms + EPS) * g_ref[...]).astype(o_ref.dtype)


def _rmsnorm(x, g, out_dtype):
    t, d = x.shape
    tm = _row_tile(t, 512)
    return pl.pallas_call(
        _rmsnorm_body,
        grid=(t // tm,),
        in_specs=[pl.BlockSpec((tm, d), lambda i: (i, 0)), pl.BlockSpec((1, d), lambda i: (0, 0))],
        out_specs=pl.BlockSpec((tm, d), lambda i: (i, 0)),
        out_shape=jax.ShapeDtypeStruct((t, d), out_dtype),
        compiler_params=_params(1, 6 * tm * d * 4),
        name="rmsnorm",
    )(x, g.reshape(1, d))


def _residual_body(x_ref, y_ref, g_ref, *rest):
    y = y_ref[...]
    ms = jnp.mean(y * y, axis=-1, keepdims=True)
    xn = x_ref[...] + y * lax.rsqrt(ms + EPS) * g_ref[...]
    if len(rest) == 1:
        rest[0][...] = xn
        return
    gn_ref, o_ref, h_ref = rest
    o_ref[...] = xn
    ms2 = jnp.mean(xn * xn, axis=-1, keepdims=True)
    h_ref[...] = (xn * lax.rsqrt(ms2 + EPS) * gn_ref[...]).astype(h_ref.dtype)


def _residual(x, y, g_post, g_next):
    t, d = x.shape
    tm = _row_tile(t, 256)
    row = pl.BlockSpec((tm, d), lambda i: (i, 0))
    vec = pl.BlockSpec((1, d), lambda i: (0, 0))
    x_out = jax.ShapeDtypeStruct((t, d), F32)
    if g_next is None:
        args, in_specs, out_specs, out_shape = (), [row, row, vec], row, x_out
    else:
        args, in_specs = (g_next.reshape(1, d),), [row, row, vec, vec]
        out_specs, out_shape = [row, row], [x_out, jax.ShapeDtypeStruct((t, d), BF16)]
    out = pl.pallas_call(
        _residual_body,
        grid=(t // tm,),
        in_specs=in_specs,
        out_specs=out_specs,
        out_shape=out_shape,
        compiler_params=_params(1, 10 * tm * d * 4),
        name="residual_norm",
    )(x, y, g_post.reshape(1, d), *args)
    return (out, None) if g_next is None else out


def _ln_gate_body(d_ref, z_ref, g_ref, b_ref, o_ref):
    d = d_ref[...]
    mu = jnp.mean(d, axis=-1, keepdims=True)
    c = d - mu
    var = jnp.mean(c * c, axis=-1, keepdims=True)
    y = c * lax.rsqrt(var + EPS) * g_ref[...] + b_ref[...]
    o_ref[...] = (_silu(y) * z_ref[...].astype(F32)).astype(o_ref.dtype)


def _ln_gate(dc, zc, g, b):
    t, e = dc.shape
    tm = _row_tile(t, 512)
    row = pl.BlockSpec((tm, e), lambda i: (i, 0))
    vec = pl.BlockSpec((1, e), lambda i: (0, 0))
    return pl.pallas_call(
        _ln_gate_body,
        grid=(t // tm,),
        in_specs=[row, row, vec, vec],
        out_specs=row,
        out_shape=jax.ShapeDtypeStruct((t, e), BF16),
        compiler_params=_params(1, 8 * tm * e * 4),
        name="ln_gate",
    )(dc, zc, g.reshape(1, e), b.reshape(1, e))


def _kv_body(m_ref, g_ref, w_ref, o_ref):
    m = m_ref[0]
    ms = jnp.mean(m * m, axis=-1, keepdims=True)
    mn = (m * lax.rsqrt(ms + EPS) * g_ref[...]).astype(BF16)
    o_ref[0] = _dot(mn, w_ref[0].astype(BF16)).astype(o_ref.dtype)


def _memory_kv(mem, g_mem, w_mem_kv, layer):
    b, n_mem, d = mem.shape
    n_out = w_mem_kv.shape[2]
    nb = min(n_out, 512)
    return pl.pallas_call(
        _kv_body,
        grid=(n_out // nb, b),
        in_specs=[pl.BlockSpec((1, n_mem, d), lambda n, bi: (bi, 0, 0)),
                  pl.BlockSpec((1, d), lambda n, bi: (0, 0)),
                  pl.BlockSpec((1, d, nb), lambda n, bi: (layer, 0, n))],
        out_specs=pl.BlockSpec((1, n_mem, nb), lambda n, bi: (bi, 0, n)),
        out_shape=jax.ShapeDtypeStruct((b, n_mem, n_out), BF16),
        compiler_params=_params(2, 2 * (n_mem * d * 4 + d * nb * 4) + d * nb * 2 + 4 * n_mem * d * 4),
        name="memory_kv",
    )(mem, g_mem.reshape(1, d), w_mem_kv)


class _WPart(NamedTuple):
    array: jax.Array
    view: Callable
    k_rows: int
    cb: int


def _cols_of(layer, col0, cb):
    return lambda ref, c, row0, rows: ref.at[layer, pl.ds(row0, rows), pl.ds(col0 + c * cb, cb)]


def _ws_call(compute, *, name, n_c, n_i, parts, blocked, blocked_specs, out_shape, out_specs,
             scratch=(), est_bytes):
    n_w, n_in, n_out = len(parts), len(blocked), len(out_shape)
    rows = [p.k_rows // n_i for p in parts]
    for p, r in zip(parts, rows):
        assert r * n_i == p.k_rows and r % BF16_ROWS == 0, (name, p.k_rows, n_i)
    n_slots = min(2, n_c)

    def body(*refs):
        ins = refs[:n_in]
        w_hbm = refs[n_in:n_in + n_w]
        outs = refs[n_in + n_w:n_in + n_w + n_out]
        rest = refs[n_in + n_w + n_out:]
        wbufs, stages, sems, user = rest[:n_w], rest[n_w:2 * n_w], rest[2 * n_w], rest[2 * n_w + 1:]
        c, i = pl.program_id(0), pl.program_id(1)

        def copy(p, cblk, k):
            src = parts[p].view(w_hbm[p], cblk, k * rows[p], rows[p])
            return pltpu.make_async_copy(src, stages[p], sems.at[p])

        def cast(p, slot, k):
            dst = pl.ds(pl.multiple_of(k * rows[p], rows[p]), rows[p])
            wbufs[p][slot, dst, :] = stages[p][...].astype(BF16)

        @pl.when(jnp.logical_and(c == 0, i == 0))
        def _():
            def load_chunk(k, carry):
                for p in range(n_w):
                    copy(p, 0, k).start()
                for p in range(n_w):
                    copy(p, 0, k).wait()
                    cast(p, 0, k)
                return carry
            lax.fori_loop(0, n_i, load_chunk, 0)

        prefetch = c + 1 < n_c

        @pl.when(prefetch)
        def _():
            for p in range(n_w):
                copy(p, c + 1, i).start()

        slot = c % n_slots
        weights = [functools.partial(lambda p: wbufs[p][slot], p) for p in range(n_w)]
        compute(c, i, ins, weights, outs, user)

        @pl.when(prefetch)
        def _():
            for p in range(n_w):
                copy(p, c + 1, i).wait()
                cast(p, (c + 1) % n_slots, i)

    w_scratch = ([pltpu.VMEM((n_slots, p.k_rows, p.cb), BF16) for p in parts]
                 + [pltpu.VMEM((r, p.cb), F32) for p, r in zip(parts, rows)]
                 + [pltpu.SemaphoreType.DMA((n_w,))])
    w_bytes = sum(n_slots * p.k_rows * p.cb * 2 + r * p.cb * 4 for p, r in zip(parts, rows))
    return pl.pallas_call(
        body,
        grid=(n_c, n_i),
        in_specs=list(blocked_specs) + [pl.BlockSpec(memory_space=pl.ANY)] * n_w,
        out_specs=out_specs,
        out_shape=out_shape,
        scratch_shapes=w_scratch + list(scratch),
        compiler_params=_params(2, w_bytes + est_bytes),
        name=name,
    )(*blocked, *[p.array for p in parts])


def _conv_a_compute(c, i, ins, weights, outs, scratch, *, tiles_per_seq, sub_rows):
    h_ref, cw_ref = ins
    wv, wb, wc, wz = weights
    o_ref, = outs
    ext_ref, = scratch
    tm, cb = o_ref.shape
    kw = cw_ref.shape[0]

    @pl.when(i % tiles_per_seq == 0)
    def _():
        ext_ref[0:CONV_HALO, :] = jnp.zeros((CONV_HALO, cb), F32)

    w_v, w_b, w_c, w_z = wv(), wb(), wc(), wz()
    for lo in range(0, tm, sub_rows):
        h = h_ref[lo:lo + sub_rows, :]
        top = CONV_HALO + lo
        ext_ref[top:top + sub_rows, :] = _dot(h, w_c) * _dot(h, w_v)
        b = _dot(h, w_b)
        z = _dot(h, w_z)
        conv = cw_ref[kw - 1:kw, :] * ext_ref[top:top + sub_rows, :]
        for k in range(kw - 1):
            back = kw - 1 - k
            conv = conv + cw_ref[k:k + 1, :] * ext_ref[top - back:top - back + sub_rows, :]
        o_ref[lo:lo + sub_rows, :] = (b * conv * _silu(z)).astype(o_ref.dtype)
    ext_ref[0:CONV_HALO, :] = ext_ref[tm:tm + CONV_HALO, :]


def _pool_compute(g, i, ins, weights, outs, scratch, *, tiles_per_seq, sub_rows):
    h_ref, ps_ref = ins
    wu, wz = weights
    o_ref, = outs
    exts = scratch
    tm, cb = o_ref.shape

    @pl.when(i % tiles_per_seq == 0)
    def _():
        for e in exts:
            e[0:POOL_HALO, :] = jnp.zeros((POOL_HALO, cb), F32)

    win = jnp.where(g == 0, POOL_WINDOWS[0],
                    jnp.where(g == 1, POOL_WINDOWS[1],
                              jnp.where(g == 2, POOL_WINDOWS[2], POOL_WINDOWS[3]))).astype(F32)
    w_u, w_z = wu(), wz()
    for lo in range(0, tm, sub_rows):
        h = h_ref[lo:lo + sub_rows, :]
        u = _dot(h, w_u)
        z = _dot(h, w_z)
        sums = []
        s = u
        top = POOL_HALO + lo
        for e, w in zip(exts, (1, 2, 4, 8)):
            e[top:top + sub_rows, :] = s
            s = s + e[top - w:top - w + sub_rows, :]
            sums.append(s)
        wsum = jnp.where(g == 0, sums[0], jnp.where(g == 1, sums[1], jnp.where(g == 2, sums[2], sums[3])))
        pos = ((i % tiles_per_seq) * tm + lo
               + lax.broadcasted_iota(jnp.int32, (sub_rows, 1), 0)).astype(F32)
        cnt = jnp.minimum(pos + 1.0, win)
        o_ref[lo:lo + sub_rows, :] = ((wsum / cnt - u) * ps_ref[...] * _silu(z)).astype(o_ref.dtype)
    for e in exts:
        e[0:POOL_HALO, :] = e[tm:tm + POOL_HALO, :]


def _conf_compute(c, i, ins, weights, outs, scratch, *, tiles_per_seq, rows_per_chunk, sub_rows):
    h_ref, cw_ref, cbias_ref = ins
    wa, wg, wz = weights
    d_ref, zc_ref = outs
    ext_ref, wrow_ref = scratch
    tm, cb = d_ref.shape
    kw = cw_ref.shape[0]
    n_sub = rows_per_chunk // SUBLANES

    @pl.when(i == 0)
    def _():
        for k in range(kw):
            wrow_ref[k] = jnp.broadcast_to(cw_ref[k:k + 1, :], (SUBLANES, cb))
        wrow_ref[kw] = jnp.broadcast_to(cbias_ref[...], (SUBLANES, cb))

    @pl.when(i % tiles_per_seq == 0)
    def _():
        for s in range(SUBLANES):
            ext_ref[s] = jnp.zeros(ext_ref.shape[1:], F32)

    w_a, w_g, w_z = wa(), wg(), wz()
    for lo in range(0, tm, sub_rows):
        h = h_ref[lo:lo + sub_rows, :]
        z = _dot(h, w_z)
        zc_ref[lo:lo + sub_rows, :] = _silu(z).astype(zc_ref.dtype)
        glu = _dot(h, w_a) * jax.nn.sigmoid(_dot(h, w_g))
        for s in range(SUBLANES):
            ext_ref[s, CONV_HALO + s + lo:CONV_HALO + s + lo + sub_rows, :] = glu
        for base in range(lo, lo + sub_rows, rows_per_chunk):
            accs = [wrow_ref[kw]] * n_sub
            for k in range(kw):
                a8, s = divmod(kw - 1 - k, SUBLANES)
                wk = wrow_ref[k]
                for j in range(n_sub):
                    row = base + CONV_HALO + SUBLANES * (j - a8)
                    accs[j] = accs[j] + wk * ext_ref[s, row:row + SUBLANES, :]
            for j in range(n_sub):
                d_ref[base + SUBLANES * j:base + SUBLANES * (j + 1), :] = accs[j]
    keep = CONV_HALO + SUBLANES
    for s in range(SUBLANES):
        ext_ref[s, 0:keep, :] = ext_ref[s, tm:tm + keep, :]


def _attn_compute(c, i, ins, weights, outs, scratch):
    h_ref, k_ref, v_ref = ins
    o_ref, = outs
    q = _dot(h_ref[...], weights[0]())
    dh = q.shape[1] // X_HEADS
    scale = dh ** -0.5
    for hd in range(X_HEADS):
        cols = slice(hd * dh, (hd + 1) * dh)
        qh = q[:, cols].astype(BF16)
        s = lax.dot_general(qh, k_ref[0, :, cols], (((1,), (1,)), ((), ())),
                            preferred_element_type=F32) * scale
        p = jnp.exp(s - jnp.max(s, axis=-1, keepdims=True))
        p = p / jnp.sum(p, axis=-1, keepdims=True)
        o_ref[:, cols] = _dot(p.astype(BF16), v_ref[0, :, cols]).astype(o_ref.dtype)


def _gates_compute(c, i, ins, weights, outs, scratch):
    outs[0][...] = jax.nn.sigmoid(_dot(ins[0][...], weights[0]())).astype(outs[0].dtype)


def _merge_compute(c, i, ins, weights, outs, scratch):
    ya_ref, pb_ref, sc_ref, att_ref, g0_ref, g1_ref, g2_ref, g3_ref = ins
    woa, wp, woc, wox = weights
    m = g0_ref[...].astype(F32) * _dot(ya_ref[...], woa())
    m = m + g1_ref[...].astype(F32) * _dot(pb_ref[...], wp())
    m = m + g2_ref[...].astype(F32) * _dot(sc_ref[...], woc())
    m = m + g3_ref[...].astype(F32) * _dot(att_ref[...], wox())
    outs[0][...] = m.astype(outs[0].dtype)


def _matmul_compute(c, i, ins, weights, outs, scratch):
    outs[0][...] = _dot(ins[0][...], weights[0]()).astype(outs[0].dtype)


def _layer(layer, h, x, kv, seq_len, w_in, conv_a_w, w_out_a, pool_scale, w_pool, conv_c_w,
           conv_c_b, ln_c_g, ln_c_b, w_out_c, w_out_x, w_o, g_post, g_next):
    t, d = h.shape
    e_a = conv_a_w.shape[1]
    e_b = pool_scale.shape[0]
    e_c = conv_c_w.shape[1]
    x_w = w_out_x.shape[1]
    n_groups, g_b, go_b = w_pool.shape[1:]
    assert n_groups == len(POOL_WINDOWS) and n_groups * go_b == d
    off_va, off_ba, off_ca, off_za = 0, e_a, 2 * e_a, 3 * e_a
    off_ub = 4 * e_a
    off_zb = off_ub + e_b
    off_ac = off_zb + e_b
    off_gc = off_ac + e_c
    off_zc = off_gc + e_c
    off_q = off_zc + e_c
    off_g = off_q + x_w
    assert off_g + 4 * d == w_in.shape[2]

    def in_part(col0, cb):
        return _WPart(w_in, _cols_of(layer, col0, cb), d, cb)

    tm_conv = _row_tile(seq_len, CONV_ROW_TILE)
    tps_conv = seq_len // tm_conv
    ni_conv = t // tm_conv
    cb_a = min(e_a, 512)
    cb_c = min(e_c, 512)
    h_row = pl.BlockSpec((tm_conv, d), lambda c, i: (i, 0))

    ka = conv_a_w.shape[0]
    ya, = _ws_call(
        functools.partial(_conv_a_compute, tiles_per_seq=tps_conv, sub_rows=min(SUB_ROWS_A, tm_conv)),
        name="branch_a", n_c=e_a // cb_a, n_i=ni_conv,
        parts=[in_part(o, cb_a) for o in (off_va, off_ba, off_ca, off_za)],
        blocked=[h, conv_a_w],
        blocked_specs=[h_row, pl.BlockSpec((ka, cb_a), lambda c, i: (0, c))],
        out_shape=[jax.ShapeDtypeStruct((t, e_a), BF16)],
        out_specs=[pl.BlockSpec((tm_conv, cb_a), lambda c, i: (i, c))],
        scratch=[pltpu.VMEM((CONV_HALO + tm_conv, cb_a), F32)],
        est_bytes=2 * tm_conv * d * 2 + 14 * tm_conv * cb_a * 4)

    pb, = _ws_call(
        functools.partial(_pool_compute, tiles_per_seq=tps_conv, sub_rows=min(SUB_ROWS_B, tm_conv)),
        name="branch_b", n_c=n_groups, n_i=ni_conv,
        parts=[in_part(o, g_b) for o in (off_ub, off_zb)],
        blocked=[h, pool_scale.reshape(1, e_b)],
        blocked_specs=[h_row, pl.BlockSpec((1, g_b), lambda c, i: (0, c))],
        out_shape=[jax.ShapeDtypeStruct((t, e_b), BF16)],
        out_specs=[pl.BlockSpec((tm_conv, g_b), lambda c, i: (i, c))],
        scratch=[pltpu.VMEM((POOL_HALO + tm_conv, g_b), F32)] * 4,
        est_bytes=2 * tm_conv * d * 2 + 18 * tm_conv * g_b * 4)

    kc = conv_c_w.shape[0]
    assert (kc - 1) // SUBLANES * SUBLANES <= CONV_HALO and off_q % x_w == 0 and off_g % go_b == 0
    sub_c = min(SUB_ROWS_C, tm_conv)
    dc, zc = _ws_call(
        functools.partial(_conf_compute, tiles_per_seq=tps_conv, rows_per_chunk=min(32, sub_c),
                          sub_rows=sub_c),
        name="branch_c", n_c=e_c // cb_c, n_i=ni_conv,
        parts=[in_part(o, cb_c) for o in (off_ac, off_gc, off_zc)],
        blocked=[h, conv_c_w, conv_c_b.reshape(1, e_c)],
        blocked_specs=[h_row, pl.BlockSpec((kc, cb_c), lambda c, i: (0, c)),
                       pl.BlockSpec((1, cb_c), lambda c, i: (0, c))],
        out_shape=[jax.ShapeDtypeStruct((t, e_c), F32), jax.ShapeDtypeStruct((t, e_c), BF16)],
        out_specs=[pl.BlockSpec((tm_conv, cb_c), lambda c, i: (i, c))] * 2,
        scratch=[pltpu.VMEM((SUBLANES, CONV_HALO + tm_conv + SUBLANES, cb_c), F32),
                 pltpu.VMEM((kc + 1, SUBLANES, cb_c), F32)],
        est_bytes=2 * tm_conv * d * 2 + 22 * tm_conv * cb_c * 4)
    sc = _ln_gate(dc, zc, ln_c_g, ln_c_b)

    tm_x = _row_tile(seq_len, CONV_ROW_TILE)
    tps_x = seq_len // tm_x
    n_mem = kv.shape[1]
    att, = _ws_call(
        _attn_compute, name="mem_attention", n_c=1, n_i=t // tm_x,
        parts=[in_part(off_q, x_w)],
        blocked=[h, kv, kv],
        blocked_specs=[pl.BlockSpec((tm_x, d), lambda c, i: (i, 0)),
                       pl.BlockSpec((1, n_mem, x_w), lambda c, i: (i // tps_x, 0, 0)),
                       pl.BlockSpec((1, n_mem, x_w), lambda c, i: (i // tps_x, 0, 1))],
        out_shape=[jax.ShapeDtypeStruct((t, x_w), BF16)],
        out_specs=[pl.BlockSpec((tm_x, x_w), lambda c, i: (i, 0))],
        est_bytes=2 * tm_x * d * 2 + 10 * tm_x * x_w * 4)

    tm_g = _row_tile(t, MATMUL_ROW_TILE)
    nb = go_b
    gates, = _ws_call(
        _gates_compute, name="merge_gates", n_c=4 * d // nb, n_i=t // tm_g,
        parts=[in_part(off_g, nb)],
        blocked=[h],
        blocked_specs=[pl.BlockSpec((tm_g, d), lambda c, i: (i, 0))],
        out_shape=[jax.ShapeDtypeStruct((t, 4 * d), BF16)],
        out_specs=[pl.BlockSpec((tm_g, nb), lambda c, i: (i, c))],
        est_bytes=2 * (tm_g * d * 2 + tm_g * nb * 2) + 3 * tm_g * nb * 4)

    tm_m = _row_tile(t, CONV_ROW_TILE)
    n_nb = d // nb
    row = lambda width: pl.BlockSpec((tm_m, width), lambda n, i: (i, 0))
    gate = lambda k: pl.BlockSpec((tm_m, nb), lambda n, i: (i, k * n_nb + n))
    merged, = _ws_call(
        _merge_compute, name="gated_merge", n_c=n_nb, n_i=t // tm_m,
        parts=[_WPart(w_out_a, _cols_of(layer, 0, nb), e_a, nb),
               _WPart(w_pool, lambda ref, c, row0, rows: ref.at[layer, c, pl.ds(row0, rows), :],
                      g_b, go_b),
               _WPart(w_out_c, _cols_of(layer, 0, nb), e_c, nb),
               _WPart(w_out_x, _cols_of(layer, 0, nb), x_w, nb)],
        blocked=[ya, pb, sc, att, gates, gates, gates, gates],
        blocked_specs=[row(e_a), pl.BlockSpec((tm_m, g_b), lambda n, i: (i, n)), row(e_c), row(x_w),
                       gate(0), gate(1), gate(2), gate(3)],
        out_shape=[jax.ShapeDtypeStruct((t, d), BF16)],
        out_specs=[pl.BlockSpec((tm_m, nb), lambda n, i: (i, n))],
        est_bytes=2 * (tm_m * (e_a + g_b + e_c + x_w) * 2 + 5 * tm_m * nb * 2) + 6 * tm_m * nb * 4)

    tm_o = _row_tile(t, MATMUL_ROW_TILE)
    y, = _ws_call(
        _matmul_compute, name="out_proj", n_c=d // nb, n_i=t // tm_o,
        parts=[_WPart(w_o, _cols_of(layer, 0, nb), d, nb)],
        blocked=[merged],
        blocked_specs=[pl.BlockSpec((tm_o, d), lambda n, i: (i, 0))],
        out_shape=[jax.ShapeDtypeStruct((t, d), F32)],
        out_specs=[pl.BlockSpec((tm_o, nb), lambda n, i: (i, n))],
        est_bytes=2 * (tm_o * d * 2 + tm_o * nb * 4) + 2 * tm_o * nb * 4)
    return _residual(x, y, g_post, g_next)


def kernel(x, mem, g_pre, g_post, g_mem, w_in, conv_a_w, w_out_a, pool_scale, w_pool, conv_c_w,
           conv_c_b, ln_c_g, ln_c_b, w_out_c, w_mem_kv, w_out_x, w_o):
    bsz, seq_len, d = x.shape
    depth = w_in.shape[0]
    xf = x.reshape(bsz * seq_len, d)
    h = _rmsnorm(xf, g_pre[0], BF16)
    for l in range(depth):
        kv = _memory_kv(mem, g_mem[l], w_mem_kv, l)
        g_next = g_pre[l + 1] if l + 1 < depth else None
        xf, h = _layer(
            l, h, xf, kv, seq_len, w_in, conv_a_w[l], w_out_a, pool_scale[l], w_pool, conv_c_w[l],
            conv_c_b[l], ln_c_g[l], ln_c_b[l], w_out_c, w_out_x, w_o, g_post[l], g_next)
    return xf.reshape(bsz, seq_len, d)
```

```python
import functools
from typing import Callable, NamedTuple

import jax
import jax.numpy as jnp
from jax import lax
from jax.experimental import pallas as pl
from jax.experimental.pallas import tpu as pltpu

EPS = 1e-6
POOL_WINDOWS = (2, 4, 8, 16)
X_HEADS = 4

SUBLANES = 8
BF16_ROWS = 16
CONV_HALO = 32
POOL_HALO = 8
V7X_VMEM_BYTES = 64 * 1024 * 1024
VMEM_CAP_BYTES = V7X_VMEM_BYTES - 6 * 1024 * 1024

CONV_ROW_TILE = 512
SUB_ROWS_A = 256
SUB_ROWS_B = 128
SUB_ROWS_C = 128
SUB_ROWS_GATES = 512
MATMUL_ROW_TILE = 1024

F32 = jnp.float32
BF16 = jnp.bfloat16


def _dot(a, b):
    return jnp.dot(a, b, preferred_element_type=F32)


def _silu(z):
    return z * jax.nn.sigmoid(z)


def _row_tile(n_rows, preferred):
    t = min(n_rows, preferred)
    while n_rows % t or t % SUBLANES:
        t -= 1
    return t


def _params(n_grid_dims, est_bytes):
    limit = int(min(max(est_bytes, 32 * 1024 * 1024), VMEM_CAP_BYTES))
    return pltpu.CompilerParams(
        dimension_semantics=("arbitrary",) * n_grid_dims, vmem_limit_bytes=limit)


def _rmsnorm_body(x_ref, g_ref, o_ref):
    x = x_ref[...]
    ms = jnp.mean(x * x, axis=-1, keepdims=True)
    o_ref[...] = (x * lax.rsqrt(ms + EPS) * g_ref[...]).astype(o_ref.dtype)


def _rmsnorm(x, g, out_dtype):
    t, d = x.shape
    tm = _row_tile(t, 512)
    return pl.pallas_call(
        _rmsnorm_body,
        grid=(t // tm,),
        in_specs=[pl.BlockSpec((tm, d), lambda i: (i, 0)), pl.BlockSpec((1, d), lambda i: (0, 0))],
        out_specs=pl.BlockSpec((tm, d), lambda i: (i, 0)),
        out_shape=jax.ShapeDtypeStruct((t, d), out_dtype),
        compiler_params=_params(1, 6 * tm * d * 4),
        name="rmsnorm",
    )(x, g.reshape(1, d))


def _residual_body(x_ref, y_ref, g_ref, *rest):
    y = y_ref[...]
    ms = jnp.mean(y * y, axis=-1, keepdims=True)
    xn = x_ref[...] + y * lax.rsqrt(ms + EPS) * g_ref[...]
    if len(rest) == 1:
        rest[0][...] = xn
        return
    gn_ref, o_ref, h_ref = rest
    o_ref[...] = xn
    ms2 = jnp.mean(xn * xn, axis=-1, keepdims=True)
    h_ref[...] = (xn * lax.rsqrt(ms2 + EPS) * gn_ref[...]).astype(h_ref.dtype)


def _residual(x, y, g_post, g_next):
    t, d = x.shape
    tm = _row_tile(t, 256)
    row = pl.BlockSpec((tm, d), lambda i: (i, 0))
    vec = pl.BlockSpec((1, d), lambda i: (0, 0))
    x_out = jax.ShapeDtypeStruct((t, d), F32)
    if g_next is None:
        args, in_specs, out_specs, out_shape = (), [row, row, vec], row, x_out
    else:
        args, in_specs = (g_next.reshape(1, d),), [row, row, vec, vec]
        out_specs, out_shape = [row, row], [x_out, jax.ShapeDtypeStruct((t, d), BF16)]
    out = pl.pallas_call(
        _residual_body,
        grid=(t // tm,),
        in_specs=in_specs,
        out_specs=out_specs,
        out_shape=out_shape,
        compiler_params=_params(1, 10 * tm * d * 4),
        name="residual_norm",
    )(x, y, g_post.reshape(1, d), *args)
    return (out, None) if g_next is None else out


def _ln_gate_body(d_ref, z_ref, g_ref, b_ref, o_ref):
    d = d_ref[...]
    mu = jnp.mean(d, axis=-1, keepdims=True)
    c = d - mu
    var = jnp.mean(c * c, axis=-1, keepdims=True)
    y = c * lax.rsqrt(var + EPS) * g_ref[...] + b_ref[...]
    o_ref[...] = (_silu(y) * z_ref[...].astype(F32)).astype(o_ref.dtype)


def _ln_gate(dc, zc, g, b):
    t, e = dc.shape
    tm = _row_tile(t, 512)
    row = pl.BlockSpec((tm, e), lambda i: (i, 0))
    vec = pl.BlockSpec((1, e), lambda i: (0, 0))
    return pl.pallas_call(
        _ln_gate_body,
        grid=(t // tm,),
        in_specs=[row, row, vec, vec],
        out_specs=row,
        out_shape=jax.ShapeDtypeStruct((t, e), BF16),
        compiler_params=_params(1, 8 * tm * e * 4),
        name="ln_gate",
    )(dc, zc, g.reshape(1, e), b.reshape(1, e))


def _kv_body(m_ref, g_ref, w_ref, o_ref):
    m = m_ref[0]
    ms = jnp.mean(m * m, axis=-1, keepdims=True)
    mn = (m * lax.rsqrt(ms + EPS) * g_ref[...]).astype(BF16)
    o_ref[0] = _dot(mn, w_ref[0].astype(BF16)).astype(o_ref.dtype)


def _memory_kv(mem, g_mem, w_mem_kv, layer):
    b, n_mem, d = mem.shape
    n_out = w_mem_kv.shape[2]
    nb = min(n_out, 512)
    return pl.pallas_call(
        _kv_body,
        grid=(n_out // nb, b),
        in_specs=[pl.BlockSpec((1, n_mem, d), lambda n, bi: (bi, 0, 0)),
                  pl.BlockSpec((1, d), lambda n, bi: (0, 0)),
                  pl.BlockSpec((1, d, nb), lambda n, bi: (layer, 0, n))],
        out_specs=pl.BlockSpec((1, n_mem, nb), lambda n, bi: (bi, 0, n)),
        out_shape=jax.ShapeDtypeStruct((b, n_mem, n_out), BF16),
        compiler_params=_params(2, 2 * (n_mem * d * 4 + d * nb * 4) + d * nb * 2 + 4 * n_mem * d * 4),
        name="memory_kv",
    )(mem, g_mem.reshape(1, d), w_mem_kv)


class _WPart(NamedTuple):
    array: jax.Array
    view: Callable
    k_rows: int
    cb: int


def _cols_of(layer, col0, cb):
    return lambda ref, c, row0, rows: ref.at[layer, pl.ds(row0, rows), pl.ds(col0 + c * cb, cb)]


def _ws_call(compute, *, name, n_c, n_i, parts, blocked, blocked_specs, out_shape, out_specs,
             scratch=(), est_bytes):
    n_w, n_in, n_out = len(parts), len(blocked), len(out_shape)
    rows = [p.k_rows // n_i for p in parts]
    for p, r in zip(parts, rows):
        assert r * n_i == p.k_rows and r % BF16_ROWS == 0 and n_i % 2 == 0, (name, p.k_rows, n_i)
    n_slots = min(2, n_c)

    def body(*refs):
        ins = refs[:n_in]
        w_hbm = refs[n_in:n_in + n_w]
        outs = refs[n_in + n_w:n_in + n_w + n_out]
        rest = refs[n_in + n_w + n_out:]
        wbufs, stages, sems, user = rest[:n_w], rest[n_w:2 * n_w], rest[2 * n_w], rest[2 * n_w + 1:]
        c, i = pl.program_id(0), pl.program_id(1)

        def copy(p, cblk, k):
            src = parts[p].view(w_hbm[p], cblk, k * rows[p], rows[p])
            return pltpu.make_async_copy(src, stages[p].at[k % 2], sems.at[p, k % 2])

        def start(cblk, k):
            for p in range(n_w):
                copy(p, cblk, k).start()

        def finish(cblk, k):
            for p in range(n_w):
                copy(p, cblk, k).wait()
                dst = pl.ds(pl.multiple_of(k * rows[p], rows[p]), rows[p])
                wbufs[p][cblk % n_slots, dst, :] = stages[p][k % 2].astype(BF16)

        @pl.when(jnp.logical_and(c == 0, i == 0))
        def _():
            start(0, 0)

            def load_chunk(k, carry):
                @pl.when(k + 1 < n_i)
                def _():
                    start(0, k + 1)
                finish(0, k)
                return carry
            lax.fori_loop(0, n_i, load_chunk, 0)
            if n_c > 1:
                start(1, 0)

        slot = c % n_slots
        weights = [functools.partial(lambda p: wbufs[p][slot], p) for p in range(n_w)]
        compute(c, i, ins, weights, outs, user)

        @pl.when(c + 1 < n_c)
        def _():
            finish(c + 1, i)

        wrap = i + 1 == n_i
        c_next = jnp.where(wrap, c + 1, c)
        i_next = jnp.where(wrap, 0, i + 1)

        @pl.when(c_next + 1 < n_c)
        def _():
            start(c_next + 1, i_next)

    w_scratch = ([pltpu.VMEM((n_slots, p.k_rows, p.cb), BF16) for p in parts]
                 + [pltpu.VMEM((2, r, p.cb), F32) for p, r in zip(parts, rows)]
                 + [pltpu.SemaphoreType.DMA((n_w, 2))])
    w_bytes = sum(n_slots * p.k_rows * p.cb * 2 + 2 * r * p.cb * 4 for p, r in zip(parts, rows))
    return pl.pallas_call(
        body,
        grid=(n_c, n_i),
        in_specs=list(blocked_specs) + [pl.BlockSpec(memory_space=pl.ANY)] * n_w,
        out_specs=out_specs,
        out_shape=out_shape,
        scratch_shapes=w_scratch + list(scratch),
        compiler_params=_params(2, w_bytes + est_bytes),
        name=name,
    )(*blocked, *[p.array for p in parts])


def _conv_a_compute(c, i, ins, weights, outs, scratch, *, tiles_per_seq, sub_rows):
    h_ref, cw_ref = ins
    wv, wb, wc, wz = weights
    o_ref, = outs
    ext_ref, = scratch
    tm, cb = o_ref.shape
    kw = cw_ref.shape[0]

    @pl.when(i % tiles_per_seq == 0)
    def _():
        ext_ref[0:CONV_HALO, :] = jnp.zeros((CONV_HALO, cb), F32)

    w_v, w_b, w_c, w_z = wv(), wb(), wc(), wz()
    for lo in range(0, tm, sub_rows):
        h = h_ref[lo:lo + sub_rows, :]
        top = CONV_HALO + lo
        ext_ref[top:top + sub_rows, :] = _dot(h, w_c) * _dot(h, w_v)
        b = _dot(h, w_b)
        z = _dot(h, w_z)
        conv = cw_ref[kw - 1:kw, :] * ext_ref[top:top + sub_rows, :]
        for k in range(kw - 1):
            back = kw - 1 - k
            conv = conv + cw_ref[k:k + 1, :] * ext_ref[top - back:top - back + sub_rows, :]
        o_ref[lo:lo + sub_rows, :] = (b * conv * _silu(z)).astype(o_ref.dtype)
    ext_ref[0:CONV_HALO, :] = ext_ref[tm:tm + CONV_HALO, :]


def _pool_compute(g, i, ins, weights, outs, scratch, *, tiles_per_seq, sub_rows):
    h_ref, ps_ref = ins
    wu, wz = weights
    o_ref, = outs
    exts = scratch
    tm, cb = o_ref.shape

    @pl.when(i % tiles_per_seq == 0)
    def _():
        for e in exts:
            e[0:POOL_HALO, :] = jnp.zeros((POOL_HALO, cb), F32)

    win = jnp.where(g == 0, POOL_WINDOWS[0],
                    jnp.where(g == 1, POOL_WINDOWS[1],
                              jnp.where(g == 2, POOL_WINDOWS[2], POOL_WINDOWS[3]))).astype(F32)
    w_u, w_z = wu(), wz()
    for lo in range(0, tm, sub_rows):
        h = h_ref[lo:lo + sub_rows, :]
        u = _dot(h, w_u)
        z = _dot(h, w_z)
        sums = []
        s = u
        top = POOL_HALO + lo
        for e, w in zip(exts, (1, 2, 4, 8)):
            e[top:top + sub_rows, :] = s
            s = s + e[top - w:top - w + sub_rows, :]
            sums.append(s)
        wsum = jnp.where(g == 0, sums[0], jnp.where(g == 1, sums[1], jnp.where(g == 2, sums[2], sums[3])))
        pos = ((i % tiles_per_seq) * tm + lo
               + lax.broadcasted_iota(jnp.int32, (sub_rows, 1), 0)).astype(F32)
        cnt = jnp.minimum(pos + 1.0, win)
        o_ref[lo:lo + sub_rows, :] = ((wsum / cnt - u) * ps_ref[...] * _silu(z)).astype(o_ref.dtype)
    for e in exts:
        e[0:POOL_HALO, :] = e[tm:tm + POOL_HALO, :]


def _conf_compute(c, i, ins, weights, outs, scratch, *, tiles_per_seq, rows_per_chunk, sub_rows):
    h_ref, cw_ref, cbias_ref = ins
    wa, wg, wz = weights
    d_ref, zc_ref = outs
    ext_ref, wrow_ref = scratch
    tm, cb = d_ref.shape
    kw = cw_ref.shape[0]
    n_sub = rows_per_chunk // SUBLANES

    @pl.when(i == 0)
    def _():
        for k in range(kw):
            wrow_ref[k] = jnp.broadcast_to(cw_ref[k:k + 1, :], (SUBLANES, cb))
        wrow_ref[kw] = jnp.broadcast_to(cbias_ref[...], (SUBLANES, cb))

    @pl.when(i % tiles_per_seq == 0)
    def _():
        for s in range(SUBLANES):
            ext_ref[s] = jnp.zeros(ext_ref.shape[1:], F32)

    w_a, w_g, w_z = wa(), wg(), wz()
    for lo in range(0, tm, sub_rows):
        h = h_ref[lo:lo + sub_rows, :]
        z = _dot(h, w_z)
        zc_ref[lo:lo + sub_rows, :] = _silu(z).astype(zc_ref.dtype)
        glu = _dot(h, w_a) * jax.nn.sigmoid(_dot(h, w_g))
        for s in range(SUBLANES):
            ext_ref[s, CONV_HALO + s + lo:CONV_HALO + s + lo + sub_rows, :] = glu
        for base in range(lo, lo + sub_rows, rows_per_chunk):
            accs = [wrow_ref[kw]] * n_sub
            for k in range(kw):
                a8, s = divmod(kw - 1 - k, SUBLANES)
                wk = wrow_ref[k]
                for j in range(n_sub):
                    row = base + CONV_HALO + SUBLANES * (j - a8)
                    accs[j] = accs[j] + wk * ext_ref[s, row:row + SUBLANES, :]
            for j in range(n_sub):
                d_ref[base + SUBLANES * j:base + SUBLANES * (j + 1), :] = accs[j]
    keep = CONV_HALO + SUBLANES
    for s in range(SUBLANES):
        ext_ref[s, 0:keep, :] = ext_ref[s, tm:tm + keep, :]


def _attn_compute(c, i, ins, weights, outs, scratch):
    h_ref, k_ref, v_ref = ins
    o_ref, = outs
    q = _dot(h_ref[...], weights[0]())
    dh = q.shape[1] // X_HEADS
    scale = dh ** -0.5
    for hd in range(X_HEADS):
        cols = slice(hd * dh, (hd + 1) * dh)
        qh = q[:, cols].astype(BF16)
        s = lax.dot_general(qh, k_ref[0, :, cols], (((1,), (1,)), ((), ())),
                            preferred_element_type=F32) * scale
        p = jnp.exp(s - jnp.max(s, axis=-1, keepdims=True))
        p = p / jnp.sum(p, axis=-1, keepdims=True)
        o_ref[:, cols] = _dot(p.astype(BF16), v_ref[0, :, cols]).astype(o_ref.dtype)


def _gates_compute(c, i, ins, weights, outs, scratch, *, sub_rows):
    h_ref, = ins
    o_ref, = outs
    w = weights[0]()
    for lo in range(0, o_ref.shape[0], sub_rows):
        o_ref[lo:lo + sub_rows, :] = jax.nn.sigmoid(_dot(h_ref[lo:lo + sub_rows, :], w)).astype(o_ref.dtype)


def _merge_compute(c, i, ins, weights, outs, scratch):
    ya_ref, pb_ref, sc_ref, att_ref, g0_ref, g1_ref, g2_ref, g3_ref = ins
    woa, wp, woc, wox = weights
    m = g0_ref[...].astype(F32) * _dot(ya_ref[...], woa())
    m = m + g1_ref[...].astype(F32) * _dot(pb_ref[...], wp())
    m = m + g2_ref[...].astype(F32) * _dot(sc_ref[...], woc())
    m = m + g3_ref[...].astype(F32) * _dot(att_ref[...], wox())
    outs[0][...] = m.astype(outs[0].dtype)


def _matmul_compute(c, i, ins, weights, outs, scratch):
    outs[0][...] = _dot(ins[0][...], weights[0]()).astype(outs[0].dtype)


def _layer(layer, h, x, kv, seq_len, w_in, conv_a_w, w_out_a, pool_scale, w_pool, conv_c_w,
           conv_c_b, ln_c_g, ln_c_b, w_out_c, w_out_x, w_o, g_post, g_next):
    t, d = h.shape
    e_a = conv_a_w.shape[1]
    e_b = pool_scale.shape[0]
    e_c = conv_c_w.shape[1]
    x_w = w_out_x.shape[1]
    n_groups, g_b, go_b = w_pool.shape[1:]
    assert n_groups == len(POOL_WINDOWS) and n_groups * go_b == d
    off_va, off_ba, off_ca, off_za = 0, e_a, 2 * e_a, 3 * e_a
    off_ub = 4 * e_a
    off_zb = off_ub + e_b
    off_ac = off_zb + e_b
    off_gc = off_ac + e_c
    off_zc = off_gc + e_c
    off_q = off_zc + e_c
    off_g = off_q + x_w
    assert off_g + 4 * d == w_in.shape[2]

    def in_part(col0, cb):
        return _WPart(w_in, _cols_of(layer, col0, cb), d, cb)

    tm_conv = _row_tile(seq_len, CONV_ROW_TILE)
    tps_conv = seq_len // tm_conv
    ni_conv = t // tm_conv
    cb_a = min(e_a, 512)
    cb_c = min(e_c, 512)
    h_row = pl.BlockSpec((tm_conv, d), lambda c, i: (i, 0))

    ka = conv_a_w.shape[0]
    ya, = _ws_call(
        functools.partial(_conv_a_compute, tiles_per_seq=tps_conv, sub_rows=min(SUB_ROWS_A, tm_conv)),
        name="branch_a", n_c=e_a // cb_a, n_i=ni_conv,
        parts=[in_part(o, cb_a) for o in (off_va, off_ba, off_ca, off_za)],
        blocked=[h, conv_a_w],
        blocked_specs=[h_row, pl.BlockSpec((ka, cb_a), lambda c, i: (0, c))],
        out_shape=[jax.ShapeDtypeStruct((t, e_a), BF16)],
        out_specs=[pl.BlockSpec((tm_conv, cb_a), lambda c, i: (i, c))],
        scratch=[pltpu.VMEM((CONV_HALO + tm_conv, cb_a), F32)],
        est_bytes=2 * tm_conv * d * 2 + 14 * tm_conv * cb_a * 4)

    pb, = _ws_call(
        functools.partial(_pool_compute, tiles_per_seq=tps_conv, sub_rows=min(SUB_ROWS_B, tm_conv)),
        name="branch_b", n_c=n_groups, n_i=ni_conv,
        parts=[in_part(o, g_b) for o in (off_ub, off_zb)],
        blocked=[h, pool_scale.reshape(1, e_b)],
        blocked_specs=[h_row, pl.BlockSpec((1, g_b), lambda c, i: (0, c))],
        out_shape=[jax.ShapeDtypeStruct((t, e_b), BF16)],
        out_specs=[pl.BlockSpec((tm_conv, g_b), lambda c, i: (i, c))],
        scratch=[pltpu.VMEM((POOL_HALO + tm_conv, g_b), F32)] * 4,
        est_bytes=2 * tm_conv * d * 2 + 18 * tm_conv * g_b * 4)

    kc = conv_c_w.shape[0]
    assert (kc - 1) // SUBLANES * SUBLANES <= CONV_HALO and off_q % x_w == 0 and off_g % go_b == 0
    sub_c = min(SUB_ROWS_C, tm_conv)
    dc, zc = _ws_call(
        functools.partial(_conf_compute, tiles_per_seq=tps_conv, rows_per_chunk=min(32, sub_c),
                          sub_rows=sub_c),
        name="branch_c", n_c=e_c // cb_c, n_i=ni_conv,
        parts=[in_part(o, cb_c) for o in (off_ac, off_gc, off_zc)],
        blocked=[h, conv_c_w, conv_c_b.reshape(1, e_c)],
        blocked_specs=[h_row, pl.BlockSpec((kc, cb_c), lambda c, i: (0, c)),
                       pl.BlockSpec((1, cb_c), lambda c, i: (0, c))],
        out_shape=[jax.ShapeDtypeStruct((t, e_c), F32), jax.ShapeDtypeStruct((t, e_c), BF16)],
        out_specs=[pl.BlockSpec((tm_conv, cb_c), lambda c, i: (i, c))] * 2,
        scratch=[pltpu.VMEM((SUBLANES, CONV_HALO + tm_conv + SUBLANES, cb_c), F32),
                 pltpu.VMEM((kc + 1, SUBLANES, cb_c), F32)],
        est_bytes=2 * tm_conv * d * 2 + 22 * tm_conv * cb_c * 4)
    sc = _ln_gate(dc, zc, ln_c_g, ln_c_b)

    tm_x = _row_tile(seq_len, CONV_ROW_TILE)
    tps_x = seq_len // tm_x
    n_mem = kv.shape[1]
    att, = _ws_call(
        _attn_compute, name="mem_attention", n_c=1, n_i=t // tm_x,
        parts=[in_part(off_q, x_w)],
        blocked=[h, kv, kv],
        blocked_specs=[pl.BlockSpec((tm_x, d), lambda c, i: (i, 0)),
                       pl.BlockSpec((1, n_mem, x_w), lambda c, i: (i // tps_x, 0, 0)),
                       pl.BlockSpec((1, n_mem, x_w), lambda c, i: (i // tps_x, 0, 1))],
        out_shape=[jax.ShapeDtypeStruct((t, x_w), BF16)],
        out_specs=[pl.BlockSpec((tm_x, x_w), lambda c, i: (i, 0))],
        est_bytes=2 * tm_x * d * 2 + 10 * tm_x * x_w * 4)

    tm_g = _row_tile(t, MATMUL_ROW_TILE)
    nb = go_b
    gates, = _ws_call(
        functools.partial(_gates_compute, sub_rows=min(SUB_ROWS_GATES, tm_g)), name="merge_gates", n_c=4 * d // nb, n_i=t // tm_g,
        parts=[in_part(off_g, nb)],
        blocked=[h],
        blocked_specs=[pl.BlockSpec((tm_g, d), lambda c, i: (i, 0))],
        out_shape=[jax.ShapeDtypeStruct((t, 4 * d), BF16)],
        out_specs=[pl.BlockSpec((tm_g, nb), lambda c, i: (i, c))],
        est_bytes=2 * (tm_g * d * 2 + tm_g * nb * 2) + 3 * tm_g * nb * 4)

    tm_m = _row_tile(t, CONV_ROW_TILE)
    n_nb = d // nb
    row = lambda width: pl.BlockSpec((tm_m, width), lambda n, i: (i, 0))
    gate = lambda k: pl.BlockSpec((tm_m, nb), lambda n, i: (i, k * n_nb + n))
    merged, = _ws_call(
        _merge_compute, name="gated_merge", n_c=n_nb, n_i=t // tm_m,
        parts=[_WPart(w_out_a, _cols_of(layer, 0, nb), e_a, nb),
               _WPart(w_pool, lambda ref, c, row0, rows: ref.at[layer, c, pl.ds(row0, rows), :],
                      g_b, go_b),
               _WPart(w_out_c, _cols_of(layer, 0, nb), e_c, nb),
               _WPart(w_out_x, _cols_of(layer, 0, nb), x_w, nb)],
        blocked=[ya, pb, sc, att, gates, gates, gates, gates],
        blocked_specs=[row(e_a), pl.BlockSpec((tm_m, g_b), lambda n, i: (i, n)), row(e_c), row(x_w),
                       gate(0), gate(1), gate(2), gate(3)],
        out_shape=[jax.ShapeDtypeStruct((t, d), BF16)],
        out_specs=[pl.BlockSpec((tm_m, nb), lambda n, i: (i, n))],
        est_bytes=2 * (tm_m * (e_a + g_b + e_c + x_w) * 2 + 5 * tm_m * nb * 2) + 6 * tm_m * nb * 4)

    tm_o = _row_tile(t, MATMUL_ROW_TILE)
    y, = _ws_call(
        _matmul_compute, name="out_proj", n_c=d // nb, n_i=t // tm_o,
        parts=[_WPart(w_o, _cols_of(layer, 0, nb), d, nb)],
        blocked=[merged],
        blocked_specs=[pl.BlockSpec((tm_o, d), lambda n, i: (i, 0))],
        out_shape=[jax.ShapeDtypeStruct((t, d), F32)],
        out_specs=[pl.BlockSpec((tm_o, nb), lambda n, i: (i, n))],
        est_bytes=2 * (tm_o * d * 2 + tm_o * nb * 4) + 2 * tm_o * nb * 4)
    return _residual(x, y, g_post, g_next)


def kernel(x, mem, g_pre, g_post, g_mem, w_in, conv_a_w, w_out_a, pool_scale, w_pool, conv_c_w,
           conv_c_b, ln_c_g, ln_c_b, w_out_c, w_mem_kv, w_out_x, w_o):
    bsz, seq_len, d = x.shape
    depth = w_in.shape[0]
    xf = x.reshape(bsz * seq_len, d)
    h = _rmsnorm(xf, g_pre[0], BF16)
    for l in range(depth):
        kv = _memory_kv(mem, g_mem[l], w_mem_kv, l)
        g_next = g_pre[l + 1] if l + 1 < depth else None
        xf, h = _layer(
            l, h, xf, kv, seq_len, w_in, conv_a_w[l], w_out_a, pool_scale[l], w_pool, conv_c_w[l],
            conv_c_b[l], ln_c_g[l], ln_c_b[l], w_out_c, w_out_x, w_o, g_post[l], g_next)
    return xf.reshape(bsz, seq_len, d)
```

```python
import functools
from typing import Callable, NamedTuple

import jax
import jax.numpy as jnp
from jax import lax
from jax.experimental import pallas as pl
from jax.experimental.pallas import tpu as pltpu

EPS = 1e-6
POOL_WINDOWS = (2, 4, 8, 16)
X_HEADS = 4

SUBLANES = 8
LANES = 128
BF16_ROWS = 16
CONV_HALO = 32
POOL_HALO = 8
V7X_VMEM_BYTES = 64 * 1024 * 1024
VMEM_CAP_BYTES = V7X_VMEM_BYTES - 6 * 1024 * 1024

CONV_ROW_TILE = 512
POOL_ROW_TILE = 1024
SUB_ROWS_A = 256
SUB_ROWS_B = 128
SUB_ROWS_C = 128
SUB_ROWS_GATES = 512
CONV_CHUNK_ROWS = 64
CONV_LANE_TILE = LANES
MATMUL_ROW_TILE = 1024

F32 = jnp.float32
BF16 = jnp.bfloat16


def _dot(a, b):
    return jnp.dot(a, b, preferred_element_type=F32)


def _silu(z):
    return z * jax.nn.sigmoid(z)


def _row_tile(n_rows, preferred):
    t = min(n_rows, preferred)
    while n_rows % t or t % SUBLANES:
        t -= 1
    return t


def _params(n_grid_dims, est_bytes):
    limit = int(min(max(est_bytes, 32 * 1024 * 1024), VMEM_CAP_BYTES))
    return pltpu.CompilerParams(
        dimension_semantics=("arbitrary",) * n_grid_dims, vmem_limit_bytes=limit)


def _rmsnorm_body(x_ref, g_ref, o_ref):
    x = x_ref[...]
    ms = jnp.mean(x * x, axis=-1, keepdims=True)
    o_ref[...] = (x * lax.rsqrt(ms + EPS) * g_ref[...]).astype(o_ref.dtype)


def _rmsnorm(x, g, out_dtype):
    t, d = x.shape
    tm = _row_tile(t, 512)
    return pl.pallas_call(
        _rmsnorm_body,
        grid=(t // tm,),
        in_specs=[pl.BlockSpec((tm, d), lambda i: (i, 0)), pl.BlockSpec((1, d), lambda i: (0, 0))],
        out_specs=pl.BlockSpec((tm, d), lambda i: (i, 0)),
        out_shape=jax.ShapeDtypeStruct((t, d), out_dtype),
        compiler_params=_params(1, 6 * tm * d * 4),
        name="rmsnorm",
    )(x, g.reshape(1, d))


def _residual_body(x_ref, y_ref, g_ref, *rest):
    y = y_ref[...]
    ms = jnp.mean(y * y, axis=-1, keepdims=True)
    xn = x_ref[...] + y * lax.rsqrt(ms + EPS) * g_ref[...]
    if len(rest) == 1:
        rest[0][...] = xn
        return
    gn_ref, o_ref, h_ref = rest
    o_ref[...] = xn
    ms2 = jnp.mean(xn * xn, axis=-1, keepdims=True)
    h_ref[...] = (xn * lax.rsqrt(ms2 + EPS) * gn_ref[...]).astype(h_ref.dtype)


def _residual(x, y, g_post, g_next):
    t, d = x.shape
    tm = _row_tile(t, 256)
    row = pl.BlockSpec((tm, d), lambda i: (i, 0))
    vec = pl.BlockSpec((1, d), lambda i: (0, 0))
    x_out = jax.ShapeDtypeStruct((t, d), F32)
    if g_next is None:
        args, in_specs, out_specs, out_shape = (), [row, row, vec], row, x_out
    else:
        args, in_specs = (g_next.reshape(1, d),), [row, row, vec, vec]
        out_specs, out_shape = [row, row], [x_out, jax.ShapeDtypeStruct((t, d), BF16)]
    out = pl.pallas_call(
        _residual_body,
        grid=(t // tm,),
        in_specs=in_specs,
        out_specs=out_specs,
        out_shape=out_shape,
        compiler_params=_params(1, 10 * tm * d * 4),
        name="residual_norm",
    )(x, y, g_post.reshape(1, d), *args)
    return (out, None) if g_next is None else out


def _ln_gate_body(d_ref, z_ref, g_ref, b_ref, o_ref):
    d = d_ref[...]
    mu = jnp.mean(d, axis=-1, keepdims=True)
    c = d - mu
    var = jnp.mean(c * c, axis=-1, keepdims=True)
    y = c * lax.rsqrt(var + EPS) * g_ref[...] + b_ref[...]
    o_ref[...] = (_silu(y) * z_ref[...].astype(F32)).astype(o_ref.dtype)


def _ln_gate(dc, zc, g, b):
    t, e = dc.shape
    tm = _row_tile(t, 512)
    row = pl.BlockSpec((tm, e), lambda i: (i, 0))
    vec = pl.BlockSpec((1, e), lambda i: (0, 0))
    return pl.pallas_call(
        _ln_gate_body,
        grid=(t // tm,),
        in_specs=[row, row, vec, vec],
        out_specs=row,
        out_shape=jax.ShapeDtypeStruct((t, e), BF16),
        compiler_params=_params(1, 8 * tm * e * 4),
        name="ln_gate",
    )(dc, zc, g.reshape(1, e), b.reshape(1, e))


def _kv_body(m_ref, g_ref, w_ref, o_ref):
    m = m_ref[0]
    ms = jnp.mean(m * m, axis=-1, keepdims=True)
    mn = (m * lax.rsqrt(ms + EPS) * g_ref[...]).astype(BF16)
    o_ref[0] = _dot(mn, w_ref[0].astype(BF16)).astype(o_ref.dtype)


def _memory_kv(mem, g_mem, w_mem_kv, layer):
    b, n_mem, d = mem.shape
    n_out = w_mem_kv.shape[2]
    nb = min(n_out, 512)
    return pl.pallas_call(
        _kv_body,
        grid=(n_out // nb, b),
        in_specs=[pl.BlockSpec((1, n_mem, d), lambda n, bi: (bi, 0, 0)),
                  pl.BlockSpec((1, d), lambda n, bi: (0, 0)),
                  pl.BlockSpec((1, d, nb), lambda n, bi: (layer, 0, n))],
        out_specs=pl.BlockSpec((1, n_mem, nb), lambda n, bi: (bi, 0, n)),
        out_shape=jax.ShapeDtypeStruct((b, n_mem, n_out), BF16),
        compiler_params=_params(2, 2 * (n_mem * d * 4 + d * nb * 4) + d * nb * 2 + 4 * n_mem * d * 4),
        name="memory_kv",
    )(mem, g_mem.reshape(1, d), w_mem_kv)


class _WPart(NamedTuple):
    array: jax.Array
    view: Callable
    k_rows: int
    cb: int


def _cols_of(layer, col0, cb):
    return lambda ref, c, row0, rows: ref.at[layer, pl.ds(row0, rows), pl.ds(col0 + c * cb, cb)]


def _ws_call(compute, *, name, n_c, n_i, parts, blocked, blocked_specs, out_shape, out_specs,
             scratch=(), est_bytes):
    n_w, n_in, n_out = len(parts), len(blocked), len(out_shape)
    rows = [p.k_rows // n_i for p in parts]
    for p, r in zip(parts, rows):
        assert r * n_i == p.k_rows and r % BF16_ROWS == 0 and n_i % 2 == 0, (name, p.k_rows, n_i)
    n_slots = min(2, n_c)

    def body(*refs):
        ins = refs[:n_in]
        w_hbm = refs[n_in:n_in + n_w]
        outs = refs[n_in + n_w:n_in + n_w + n_out]
        rest = refs[n_in + n_w + n_out:]
        wbufs, stages, sems, user = rest[:n_w], rest[n_w:2 * n_w], rest[2 * n_w], rest[2 * n_w + 1:]
        c, i = pl.program_id(0), pl.program_id(1)

        def copy(p, cblk, k):
            src = parts[p].view(w_hbm[p], cblk, k * rows[p], rows[p])
            return pltpu.make_async_copy(src, stages[p].at[k % 2], sems.at[p, k % 2])

        def start(cblk, k):
            for p in range(n_w):
                copy(p, cblk, k).start()

        def finish(cblk, k):
            for p in range(n_w):
                copy(p, cblk, k).wait()
                dst = pl.ds(pl.multiple_of(k * rows[p], rows[p]), rows[p])
                wbufs[p][cblk % n_slots, dst, :] = stages[p][k % 2].astype(BF16)

        @pl.when(jnp.logical_and(c == 0, i == 0))
        def _():
            start(0, 0)

            def load_chunk(k, carry):
                @pl.when(k + 1 < n_i)
                def _():
                    start(0, k + 1)
                finish(0, k)
                return carry
            lax.fori_loop(0, n_i, load_chunk, 0)
            if n_c > 1:
                start(1, 0)

        slot = c % n_slots
        weights = [functools.partial(lambda p: wbufs[p][slot], p) for p in range(n_w)]
        compute(c, i, ins, weights, outs, user)

        @pl.when(c + 1 < n_c)
        def _():
            finish(c + 1, i)

        wrap = i + 1 == n_i
        c_next = jnp.where(wrap, c + 1, c)
        i_next = jnp.where(wrap, 0, i + 1)

        @pl.when(c_next + 1 < n_c)
        def _():
            start(c_next + 1, i_next)

    w_scratch = ([pltpu.VMEM((n_slots, p.k_rows, p.cb), BF16) for p in parts]
                 + [pltpu.VMEM((2, r, p.cb), F32) for p, r in zip(parts, rows)]
                 + [pltpu.SemaphoreType.DMA((n_w, 2))])
    w_bytes = sum(n_slots * p.k_rows * p.cb * 2 + 2 * r * p.cb * 4 for p, r in zip(parts, rows))
    return pl.pallas_call(
        body,
        grid=(n_c, n_i),
        in_specs=list(blocked_specs) + [pl.BlockSpec(memory_space=pl.ANY)] * n_w,
        out_specs=out_specs,
        out_shape=out_shape,
        scratch_shapes=w_scratch + list(scratch),
        compiler_params=_params(2, w_bytes + est_bytes),
        name=name,
    )(*blocked, *[p.array for p in parts])


def _conv_a_compute(c, i, ins, weights, outs, scratch, *, tiles_per_seq, sub_rows):
    h_ref, cw_ref = ins
    wv, wb, wc, wz = weights
    o_ref, = outs
    ext_ref, = scratch
    tm, cb = o_ref.shape
    kw = cw_ref.shape[0]

    @pl.when(i % tiles_per_seq == 0)
    def _():
        ext_ref[0:CONV_HALO, :] = jnp.zeros((CONV_HALO, cb), F32)

    w_v, w_b, w_c, w_z = wv(), wb(), wc(), wz()
    for lo in range(0, tm, sub_rows):
        h = h_ref[lo:lo + sub_rows, :]
        top = CONV_HALO + lo
        ext_ref[top:top + sub_rows, :] = _dot(h, w_c) * _dot(h, w_v)
        b = _dot(h, w_b)
        z = _dot(h, w_z)
        conv = cw_ref[kw - 1:kw, :] * ext_ref[top:top + sub_rows, :]
        for k in range(kw - 1):
            back = kw - 1 - k
            conv = conv + cw_ref[k:k + 1, :] * ext_ref[top - back:top - back + sub_rows, :]
        o_ref[lo:lo + sub_rows, :] = (b * conv * _silu(z)).astype(o_ref.dtype)
    ext_ref[0:CONV_HALO, :] = ext_ref[tm:tm + CONV_HALO, :]


def _pool_compute(g, i, ins, weights, outs, scratch, *, tiles_per_seq, sub_rows):
    h_ref, ps_ref = ins
    wu, wz = weights
    o_ref, = outs
    exts = scratch
    tm, cb = o_ref.shape

    @pl.when(i % tiles_per_seq == 0)
    def _():
        for e in exts:
            e[0:POOL_HALO, :] = jnp.zeros((POOL_HALO, cb), F32)

    win = jnp.where(g == 0, POOL_WINDOWS[0],
                    jnp.where(g == 1, POOL_WINDOWS[1],
                              jnp.where(g == 2, POOL_WINDOWS[2], POOL_WINDOWS[3]))).astype(F32)
    w_u, w_z = wu(), wz()
    for lo in range(0, tm, sub_rows):
        h = h_ref[lo:lo + sub_rows, :]
        u = _dot(h, w_u)
        z = _dot(h, w_z)
        sums = []
        s = u
        top = POOL_HALO + lo
        for e, w in zip(exts, (1, 2, 4, 8)):
            e[top:top + sub_rows, :] = s
            s = s + e[top - w:top - w + sub_rows, :]
            sums.append(s)
        wsum = jnp.where(g == 0, sums[0], jnp.where(g == 1, sums[1], jnp.where(g == 2, sums[2], sums[3])))
        pos = ((i % tiles_per_seq) * tm + lo
               + lax.broadcasted_iota(jnp.int32, (sub_rows, 1), 0)).astype(F32)
        cnt = jnp.minimum(pos + 1.0, win)
        o_ref[lo:lo + sub_rows, :] = ((wsum / cnt - u) * ps_ref[...] * _silu(z)).astype(o_ref.dtype)
    for e in exts:
        e[0:POOL_HALO, :] = e[tm:tm + POOL_HALO, :]


def _conf_compute(c, i, ins, weights, outs, scratch, *, tiles_per_seq, rows_per_chunk, sub_rows):
    h_ref, cw_ref, cbias_ref = ins
    wa, wg, wz = weights
    d_ref, zc_ref = outs
    ext_ref, wrow_ref, brow_ref, odd_ref = scratch
    tm, cb = d_ref.shape
    kw = cw_ref.shape[0]
    grp = BF16_ROWS
    n_grp = rows_per_chunk // grp

    @pl.when(i == 0)
    def _():
        for k in range(kw):
            wrow_ref[k] = jnp.broadcast_to(cw_ref[k:k + 1, :], (grp, cb)).astype(wrow_ref.dtype)
        brow_ref[...] = jnp.broadcast_to(cbias_ref[...], (grp, cb))

    @pl.when(i % tiles_per_seq == 0)
    def _():
        for e in range(ext_ref.shape[0]):
            ext_ref[e] = jnp.zeros(ext_ref.shape[1:], ext_ref.dtype)
        odd_ref[0:grp, :] = jnp.zeros((grp, cb), F32)

    w_a, w_g, w_z = wa(), wg(), wz()
    for lo in range(0, tm, sub_rows):
        h = h_ref[lo:lo + sub_rows, :]
        z = _dot(h, w_z)
        zc_ref[lo:lo + sub_rows, :] = _silu(z).astype(zc_ref.dtype)
        glu = (_dot(h, w_a) * jax.nn.sigmoid(_dot(h, w_g))).astype(ext_ref.dtype)
        for e in range(ext_ref.shape[0]):
            top = CONV_HALO + 2 * e + lo
            ext_ref[e, top:top + sub_rows, :] = glu

    lane_tile = min(cb, CONV_LANE_TILE)

    def conv_chunk(r, carry):
        base = pl.multiple_of(r * rows_per_chunk, rows_per_chunk)
        for lane0 in range(0, cb, lane_tile):
            lanes = slice(lane0, lane0 + lane_tile)
            sums = []
            for parity in (0, 1):
                accs = [None] * n_grp
                for k in range(kw):
                    blocks, s = divmod(kw - 1 - k, grp)
                    if s % 2 != parity:
                        continue
                    for j in range(n_grp):
                        row = base + (CONV_HALO + grp * (j - blocks))
                        term = (ext_ref[s // 2, pl.ds(row, grp), lanes].astype(F32)
                                * wrow_ref[k, :, lanes].astype(F32))
                        accs[j] = term if accs[j] is None else accs[j] + term
                sums.append(accs)
            even, odd = sums
            for j in range(n_grp):
                odd_ref[grp * (j + 1):grp * (j + 2), lanes] = odd[j]
            for j in range(n_grp):
                d_ref[pl.ds(base + grp * j, grp), lanes] = (
                    even[j] + brow_ref[:, lanes]
                    + odd_ref[grp * (j + 1) - 1:grp * (j + 2) - 1, lanes])
            odd_ref[0:grp, lanes] = odd[n_grp - 1]
        return carry

    lax.fori_loop(0, tm // rows_per_chunk, conv_chunk, 0)
    keep = CONV_HALO + grp
    for e in range(ext_ref.shape[0]):
        ext_ref[e, 0:keep, :] = ext_ref[e, tm:tm + keep, :]


def _attn_compute(c, i, ins, weights, outs, scratch):
    h_ref, k_ref, v_ref = ins
    o_ref, = outs
    q = _dot(h_ref[...], weights[0]())
    dh = q.shape[1] // X_HEADS
    scale = dh ** -0.5
    for hd in range(X_HEADS):
        cols = slice(hd * dh, (hd + 1) * dh)
        qh = q[:, cols].astype(BF16)
        s = lax.dot_general(qh, k_ref[0, :, cols], (((1,), (1,)), ((), ())),
                            preferred_element_type=F32) * scale
        p = jnp.exp(s - jnp.max(s, axis=-1, keepdims=True))
        p = p / jnp.sum(p, axis=-1, keepdims=True)
        o_ref[:, cols] = _dot(p.astype(BF16), v_ref[0, :, cols]).astype(o_ref.dtype)


def _gates_compute(c, i, ins, weights, outs, scratch, *, sub_rows):
    h_ref, = ins
    o_ref, = outs
    w = weights[0]()
    for lo in range(0, o_ref.shape[0], sub_rows):
        o_ref[lo:lo + sub_rows, :] = jax.nn.sigmoid(_dot(h_ref[lo:lo + sub_rows, :], w)).astype(o_ref.dtype)


def _merge_compute(c, i, ins, weights, outs, scratch):
    ya_ref, pb_ref, sc_ref, att_ref, g0_ref, g1_ref, g2_ref, g3_ref = ins
    woa, wp, woc, wox = weights
    m = g0_ref[...].astype(F32) * _dot(ya_ref[...], woa())
    m = m + g1_ref[...].astype(F32) * _dot(pb_ref[...], wp())
    m = m + g2_ref[...].astype(F32) * _dot(sc_ref[...], woc())
    m = m + g3_ref[...].astype(F32) * _dot(att_ref[...], wox())
    outs[0][...] = m.astype(outs[0].dtype)


def _matmul_compute(c, i, ins, weights, outs, scratch):
    outs[0][...] = _dot(ins[0][...], weights[0]()).astype(outs[0].dtype)


def _layer(layer, h, x, kv, seq_len, w_in, conv_a_w, w_out_a, pool_scale, w_pool, conv_c_w,
           conv_c_b, ln_c_g, ln_c_b, w_out_c, w_out_x, w_o, g_post, g_next):
    t, d = h.shape
    e_a = conv_a_w.shape[1]
    e_b = pool_scale.shape[0]
    e_c = conv_c_w.shape[1]
    x_w = w_out_x.shape[1]
    n_groups, g_b, go_b = w_pool.shape[1:]
    assert n_groups == len(POOL_WINDOWS) and n_groups * go_b == d
    off_va, off_ba, off_ca, off_za = 0, e_a, 2 * e_a, 3 * e_a
    off_ub = 4 * e_a
    off_zb = off_ub + e_b
    off_ac = off_zb + e_b
    off_gc = off_ac + e_c
    off_zc = off_gc + e_c
    off_q = off_zc + e_c
    off_g = off_q + x_w
    assert off_g + 4 * d == w_in.shape[2]

    def in_part(col0, cb):
        return _WPart(w_in, _cols_of(layer, col0, cb), d, cb)

    tm_conv = _row_tile(seq_len, CONV_ROW_TILE)
    tps_conv = seq_len // tm_conv
    ni_conv = t // tm_conv
    cb_a = min(e_a, 512)
    cb_c = min(e_c, 512)
    h_row = pl.BlockSpec((tm_conv, d), lambda c, i: (i, 0))

    ka = conv_a_w.shape[0]
    ya, = _ws_call(
        functools.partial(_conv_a_compute, tiles_per_seq=tps_conv, sub_rows=min(SUB_ROWS_A, tm_conv)),
        name="branch_a", n_c=e_a // cb_a, n_i=ni_conv,
        parts=[in_part(o, cb_a) for o in (off_va, off_ba, off_ca, off_za)],
        blocked=[h, conv_a_w],
        blocked_specs=[h_row, pl.BlockSpec((ka, cb_a), lambda c, i: (0, c))],
        out_shape=[jax.ShapeDtypeStruct((t, e_a), BF16)],
        out_specs=[pl.BlockSpec((tm_conv, cb_a), lambda c, i: (i, c))],
        scratch=[pltpu.VMEM((CONV_HALO + tm_conv, cb_a), F32)],
        est_bytes=2 * tm_conv * d * 2 + 14 * tm_conv * cb_a * 4)

    tm_b = _row_tile(seq_len, POOL_ROW_TILE)
    pb, = _ws_call(
        functools.partial(_pool_compute, tiles_per_seq=seq_len // tm_b, sub_rows=min(SUB_ROWS_B, tm_b)),
        name="branch_b", n_c=n_groups, n_i=t // tm_b,
        parts=[in_part(o, g_b) for o in (off_ub, off_zb)],
        blocked=[h, pool_scale.reshape(1, e_b)],
        blocked_specs=[pl.BlockSpec((tm_b, d), lambda c, i: (i, 0)),
                       pl.BlockSpec((1, g_b), lambda c, i: (0, c))],
        out_shape=[jax.ShapeDtypeStruct((t, e_b), BF16)],
        out_specs=[pl.BlockSpec((tm_b, g_b), lambda c, i: (i, c))],
        scratch=[pltpu.VMEM((POOL_HALO + tm_b, g_b), F32)] * 4,
        est_bytes=2 * tm_b * d * 2 + 14 * tm_b * g_b * 4)

    kc = conv_c_w.shape[0]
    assert (kc - 1) // BF16_ROWS * BF16_ROWS <= CONV_HALO and off_q % x_w == 0 and off_g % go_b == 0
    sub_c = min(SUB_ROWS_C, tm_conv)
    chunk_c = min(CONV_CHUNK_ROWS, sub_c)
    dc, zc = _ws_call(
        functools.partial(_conf_compute, tiles_per_seq=tps_conv, rows_per_chunk=chunk_c, sub_rows=sub_c),
        name="branch_c", n_c=e_c // cb_c, n_i=ni_conv,
        parts=[in_part(o, cb_c) for o in (off_ac, off_gc, off_zc)],
        blocked=[h, conv_c_w, conv_c_b.reshape(1, e_c)],
        blocked_specs=[h_row, pl.BlockSpec((kc, cb_c), lambda c, i: (0, c)),
                       pl.BlockSpec((1, cb_c), lambda c, i: (0, c))],
        out_shape=[jax.ShapeDtypeStruct((t, e_c), F32), jax.ShapeDtypeStruct((t, e_c), BF16)],
        out_specs=[pl.BlockSpec((tm_conv, cb_c), lambda c, i: (i, c))] * 2,
        scratch=[pltpu.VMEM((BF16_ROWS // 2, CONV_HALO + tm_conv + BF16_ROWS, cb_c), BF16),
                 pltpu.VMEM((kc, BF16_ROWS, cb_c), BF16),
                 pltpu.VMEM((BF16_ROWS, cb_c), F32),
                 pltpu.VMEM((BF16_ROWS + chunk_c, cb_c), F32)],
        est_bytes=2 * tm_conv * d * 2 + 22 * tm_conv * cb_c * 4)
    sc = _ln_gate(dc, zc, ln_c_g, ln_c_b)

    tm_x = _row_tile(seq_len, CONV_ROW_TILE)
    tps_x = seq_len // tm_x
    n_mem = kv.shape[1]
    att, = _ws_call(
        _attn_compute, name="mem_attention", n_c=1, n_i=t // tm_x,
        parts=[in_part(off_q, x_w)],
        blocked=[h, kv, kv],
        blocked_specs=[pl.BlockSpec((tm_x, d), lambda c, i: (i, 0)),
                       pl.BlockSpec((1, n_mem, x_w), lambda c, i: (i // tps_x, 0, 0)),
                       pl.BlockSpec((1, n_mem, x_w), lambda c, i: (i // tps_x, 0, 1))],
        out_shape=[jax.ShapeDtypeStruct((t, x_w), BF16)],
        out_specs=[pl.BlockSpec((tm_x, x_w), lambda c, i: (i, 0))],
        est_bytes=2 * tm_x * d * 2 + 10 * tm_x * x_w * 4)

    tm_g = _row_tile(t, MATMUL_ROW_TILE)
    nb = go_b
    gates, = _ws_call(
        functools.partial(_gates_compute, sub_rows=min(SUB_ROWS_GATES, tm_g)),
        name="merge_gates", n_c=4 * d // nb, n_i=t // tm_g,
        parts=[in_part(off_g, nb)],
        blocked=[h],
        blocked_specs=[pl.BlockSpec((tm_g, d), lambda c, i: (i, 0))],
        out_shape=[jax.ShapeDtypeStruct((t, 4 * d), BF16)],
        out_specs=[pl.BlockSpec((tm_g, nb), lambda c, i: (i, c))],
        est_bytes=2 * (tm_g * d * 2 + tm_g * nb * 2) + 3 * tm_g * nb * 4)

    tm_m = _row_tile(t, CONV_ROW_TILE)
    n_nb = d // nb
    row = lambda width: pl.BlockSpec((tm_m, width), lambda n, i: (i, 0))
    gate = lambda k: pl.BlockSpec((tm_m, nb), lambda n, i: (i, k * n_nb + n))
    merged, = _ws_call(
        _merge_compute, name="gated_merge", n_c=n_nb, n_i=t // tm_m,
        parts=[_WPart(w_out_a, _cols_of(layer, 0, nb), e_a, nb),
               _WPart(w_pool, lambda ref, c, row0, rows: ref.at[layer, c, pl.ds(row0, rows), :],
                      g_b, go_b),
               _WPart(w_out_c, _cols_of(layer, 0, nb), e_c, nb),
               _WPart(w_out_x, _cols_of(layer, 0, nb), x_w, nb)],
        blocked=[ya, pb, sc, att, gates, gates, gates, gates],
        blocked_specs=[row(e_a), pl.BlockSpec((tm_m, g_b), lambda n, i: (i, n)), row(e_c), row(x_w),
                       gate(0), gate(1), gate(2), gate(3)],
        out_shape=[jax.ShapeDtypeStruct((t, d), BF16)],
        out_specs=[pl.BlockSpec((tm_m, nb), lambda n, i: (i, n))],
        est_bytes=2 * (tm_m * (e_a + g_b + e_c + x_w) * 2 + 5 * tm_m * nb * 2) + 6 * tm_m * nb * 4)

    tm_o = _row_tile(t, MATMUL_ROW_TILE)
    y, = _ws_call(
        _matmul_compute, name="out_proj", n_c=d // nb, n_i=t // tm_o,
        parts=[_WPart(w_o, _cols_of(layer, 0, nb), d, nb)],
        blocked=[merged],
        blocked_specs=[pl.BlockSpec((tm_o, d), lambda n, i: (i, 0))],
        out_shape=[jax.ShapeDtypeStruct((t, d), F32)],
        out_specs=[pl.BlockSpec((tm_o, nb), lambda n, i: (i, n))],
        est_bytes=2 * (tm_o * d * 2 + tm_o * nb * 4) + 2 * tm_o * nb * 4)
    return _residual(x, y, g_post, g_next)


def kernel(x, mem, g_pre, g_post, g_mem, w_in, conv_a_w, w_out_a, pool_scale, w_pool, conv_c_w,
           conv_c_b, ln_c_g, ln_c_b, w_out_c, w_mem_kv, w_out_x, w_o):
    bsz, seq_len, d = x.shape
    depth = w_in.shape[0]
    xf = x.reshape(bsz * seq_len, d)
    h = _rmsnorm(xf, g_pre[0], BF16)
    for l in range(depth):
        kv = _memory_kv(mem, g_mem[l], w_mem_kv, l)
        g_next = g_pre[l + 1] if l + 1 < depth else None
        xf, h = _layer(
            l, h, xf, kv, seq_len, w_in, conv_a_w[l], w_out_a, pool_scale[l], w_pool, conv_c_w[l],
            conv_c_b[l], ln_c_g[l], ln_c_b[l], w_out_c, w_out_x, w_o, g_post[l], g_next)
    return xf.reshape(bsz, seq_len, d)
```

```python
import functools
from typing import Callable, NamedTuple

import jax
import jax.numpy as jnp
from jax import lax
from jax.experimental import pallas as pl
from jax.experimental.pallas import tpu as pltpu

EPS = 1e-6
POOL_WINDOWS = (2, 4, 8, 16)
X_HEADS = 4

SUBLANES = 8
LANES = 128
BF16_ROWS = 16
CONV_HALO = 32
POOL_HALO = 8
V7X_VMEM_BYTES = 64 * 1024 * 1024
VMEM_CAP_BYTES = V7X_VMEM_BYTES - 6 * 1024 * 1024

CONV_ROW_TILE = 512
POOL_ROW_TILE = 1024
SUB_ROWS_A = 256
SUB_ROWS_B = 128
SUB_ROWS_C = 128
SUB_ROWS_GATES = 512
CONV_CHUNK_ROWS = 256
CONV_LANE_TILE = LANES
MATMUL_ROW_TILE = 1024

F32 = jnp.float32
BF16 = jnp.bfloat16


def _dot(a, b):
    return jnp.dot(a, b, preferred_element_type=F32)


def _silu(z):
    return z * jax.nn.sigmoid(z)


def _row_tile(n_rows, preferred):
    t = min(n_rows, preferred)
    while n_rows % t or t % SUBLANES:
        t -= 1
    return t


def _params(n_grid_dims, est_bytes):
    limit = int(min(max(est_bytes, 32 * 1024 * 1024), VMEM_CAP_BYTES))
    return pltpu.CompilerParams(
        dimension_semantics=("arbitrary",) * n_grid_dims, vmem_limit_bytes=limit)


def _rmsnorm_body(x_ref, g_ref, o_ref):
    x = x_ref[...]
    ms = jnp.mean(x * x, axis=-1, keepdims=True)
    o_ref[...] = (x * lax.rsqrt(ms + EPS) * g_ref[...]).astype(o_ref.dtype)


def _rmsnorm(x, g, out_dtype):
    t, d = x.shape
    tm = _row_tile(t, 512)
    return pl.pallas_call(
        _rmsnorm_body,
        grid=(t // tm,),
        in_specs=[pl.BlockSpec((tm, d), lambda i: (i, 0)), pl.BlockSpec((1, d), lambda i: (0, 0))],
        out_specs=pl.BlockSpec((tm, d), lambda i: (i, 0)),
        out_shape=jax.ShapeDtypeStruct((t, d), out_dtype),
        compiler_params=_params(1, 6 * tm * d * 4),
        name="rmsnorm",
    )(x, g.reshape(1, d))


def _residual_body(x_ref, y_ref, g_ref, *rest):
    y = y_ref[...]
    ms = jnp.mean(y * y, axis=-1, keepdims=True)
    xn = x_ref[...] + y * lax.rsqrt(ms + EPS) * g_ref[...]
    if len(rest) == 1:
        rest[0][...] = xn
        return
    gn_ref, o_ref, h_ref = rest
    o_ref[...] = xn
    ms2 = jnp.mean(xn * xn, axis=-1, keepdims=True)
    h_ref[...] = (xn * lax.rsqrt(ms2 + EPS) * gn_ref[...]).astype(h_ref.dtype)


def _residual(x, y, g_post, g_next):
    t, d = x.shape
    tm = _row_tile(t, 256)
    row = pl.BlockSpec((tm, d), lambda i: (i, 0))
    vec = pl.BlockSpec((1, d), lambda i: (0, 0))
    x_out = jax.ShapeDtypeStruct((t, d), F32)
    if g_next is None:
        args, in_specs, out_specs, out_shape = (), [row, row, vec], row, x_out
    else:
        args, in_specs = (g_next.reshape(1, d),), [row, row, vec, vec]
        out_specs, out_shape = [row, row], [x_out, jax.ShapeDtypeStruct((t, d), BF16)]
    out = pl.pallas_call(
        _residual_body,
        grid=(t // tm,),
        in_specs=in_specs,
        out_specs=out_specs,
        out_shape=out_shape,
        compiler_params=_params(1, 10 * tm * d * 4),
        name="residual_norm",
    )(x, y, g_post.reshape(1, d), *args)
    return (out, None) if g_next is None else out


def _ln_gate_body(d_ref, z_ref, g_ref, b_ref, o_ref):
    d = d_ref[...]
    mu = jnp.mean(d, axis=-1, keepdims=True)
    c = d - mu
    var = jnp.mean(c * c, axis=-1, keepdims=True)
    y = c * lax.rsqrt(var + EPS) * g_ref[...] + b_ref[...]
    o_ref[...] = (_silu(y) * z_ref[...].astype(F32)).astype(o_ref.dtype)


def _ln_gate(dc, zc, g, b):
    t, e = dc.shape
    tm = _row_tile(t, 512)
    row = pl.BlockSpec((tm, e), lambda i: (i, 0))
    vec = pl.BlockSpec((1, e), lambda i: (0, 0))
    return pl.pallas_call(
        _ln_gate_body,
        grid=(t // tm,),
        in_specs=[row, row, vec, vec],
        out_specs=row,
        out_shape=jax.ShapeDtypeStruct((t, e), BF16),
        compiler_params=_params(1, 8 * tm * e * 4),
        name="ln_gate",
    )(dc, zc, g.reshape(1, e), b.reshape(1, e))


def _kv_body(m_ref, g_ref, w_ref, o_ref):
    m = m_ref[0]
    ms = jnp.mean(m * m, axis=-1, keepdims=True)
    mn = (m * lax.rsqrt(ms + EPS) * g_ref[...]).astype(BF16)
    o_ref[0] = _dot(mn, w_ref[0].astype(BF16)).astype(o_ref.dtype)


def _memory_kv(mem, g_mem, w_mem_kv, layer):
    b, n_mem, d = mem.shape
    n_out = w_mem_kv.shape[2]
    nb = min(n_out, 512)
    return pl.pallas_call(
        _kv_body,
        grid=(n_out // nb, b),
        in_specs=[pl.BlockSpec((1, n_mem, d), lambda n, bi: (bi, 0, 0)),
                  pl.BlockSpec((1, d), lambda n, bi: (0, 0)),
                  pl.BlockSpec((1, d, nb), lambda n, bi: (layer, 0, n))],
        out_specs=pl.BlockSpec((1, n_mem, nb), lambda n, bi: (bi, 0, n)),
        out_shape=jax.ShapeDtypeStruct((b, n_mem, n_out), BF16),
        compiler_params=_params(2, 2 * (n_mem * d * 4 + d * nb * 4) + d * nb * 2 + 4 * n_mem * d * 4),
        name="memory_kv",
    )(mem, g_mem.reshape(1, d), w_mem_kv)


class _WPart(NamedTuple):
    array: jax.Array
    view: Callable
    k_rows: int
    cb: int


def _cols_of(layer, col0, cb):
    return lambda ref, c, row0, rows: ref.at[layer, pl.ds(row0, rows), pl.ds(col0 + c * cb, cb)]


def _ws_call(compute, *, name, n_c, n_i, parts, blocked, blocked_specs, out_shape, out_specs,
             scratch=(), est_bytes):
    n_w, n_in, n_out = len(parts), len(blocked), len(out_shape)
    rows = [p.k_rows // n_i for p in parts]
    for p, r in zip(parts, rows):
        assert r * n_i == p.k_rows and r % BF16_ROWS == 0 and n_i % 2 == 0, (name, p.k_rows, n_i)
    n_slots = min(2, n_c)

    def body(*refs):
        ins = refs[:n_in]
        w_hbm = refs[n_in:n_in + n_w]
        outs = refs[n_in + n_w:n_in + n_w + n_out]
        rest = refs[n_in + n_w + n_out:]
        wbufs, stages, sems, user = rest[:n_w], rest[n_w:2 * n_w], rest[2 * n_w], rest[2 * n_w + 1:]
        c, i = pl.program_id(0), pl.program_id(1)

        def copy(p, cblk, k):
            src = parts[p].view(w_hbm[p], cblk, k * rows[p], rows[p])
            return pltpu.make_async_copy(src, stages[p].at[k % 2], sems.at[p, k % 2])

        def start(cblk, k):
            for p in range(n_w):
                copy(p, cblk, k).start()

        def finish(cblk, k):
            for p in range(n_w):
                copy(p, cblk, k).wait()
                dst = pl.ds(pl.multiple_of(k * rows[p], rows[p]), rows[p])
                wbufs[p][cblk % n_slots, dst, :] = stages[p][k % 2].astype(BF16)

        @pl.when(jnp.logical_and(c == 0, i == 0))
        def _():
            start(0, 0)

            def load_chunk(k, carry):
                @pl.when(k + 1 < n_i)
                def _():
                    start(0, k + 1)
                finish(0, k)
                return carry
            lax.fori_loop(0, n_i, load_chunk, 0)
            if n_c > 1:
                start(1, 0)

        slot = c % n_slots
        weights = [functools.partial(lambda p: wbufs[p][slot], p) for p in range(n_w)]
        compute(c, i, ins, weights, outs, user)

        @pl.when(c + 1 < n_c)
        def _():
            finish(c + 1, i)

        wrap = i + 1 == n_i
        c_next = jnp.where(wrap, c + 1, c)
        i_next = jnp.where(wrap, 0, i + 1)

        @pl.when(c_next + 1 < n_c)
        def _():
            start(c_next + 1, i_next)

    w_scratch = ([pltpu.VMEM((n_slots, p.k_rows, p.cb), BF16) for p in parts]
                 + [pltpu.VMEM((2, r, p.cb), F32) for p, r in zip(parts, rows)]
                 + [pltpu.SemaphoreType.DMA((n_w, 2))])
    w_bytes = sum(n_slots * p.k_rows * p.cb * 2 + 2 * r * p.cb * 4 for p, r in zip(parts, rows))
    return pl.pallas_call(
        body,
        grid=(n_c, n_i),
        in_specs=list(blocked_specs) + [pl.BlockSpec(memory_space=pl.ANY)] * n_w,
        out_specs=out_specs,
        out_shape=out_shape,
        scratch_shapes=w_scratch + list(scratch),
        compiler_params=_params(2, w_bytes + est_bytes),
        name=name,
    )(*blocked, *[p.array for p in parts])


def _conv_a_compute(c, i, ins, weights, outs, scratch, *, tiles_per_seq, sub_rows):
    h_ref, cw_ref = ins
    wv, wb, wc, wz = weights
    o_ref, = outs
    ext_ref, = scratch
    tm, cb = o_ref.shape
    kw = cw_ref.shape[0]

    @pl.when(i % tiles_per_seq == 0)
    def _():
        ext_ref[0:CONV_HALO, :] = jnp.zeros((CONV_HALO, cb), F32)

    w_v, w_b, w_c, w_z = wv(), wb(), wc(), wz()
    for lo in range(0, tm, sub_rows):
        h = h_ref[lo:lo + sub_rows, :]
        top = CONV_HALO + lo
        ext_ref[top:top + sub_rows, :] = _dot(h, w_c) * _dot(h, w_v)
        b = _dot(h, w_b)
        z = _dot(h, w_z)
        conv = cw_ref[kw - 1:kw, :] * ext_ref[top:top + sub_rows, :]
        for k in range(kw - 1):
            back = kw - 1 - k
            conv = conv + cw_ref[k:k + 1, :] * ext_ref[top - back:top - back + sub_rows, :]
        o_ref[lo:lo + sub_rows, :] = (b * conv * _silu(z)).astype(o_ref.dtype)
    ext_ref[0:CONV_HALO, :] = ext_ref[tm:tm + CONV_HALO, :]


def _pool_compute(g, i, ins, weights, outs, scratch, *, tiles_per_seq, sub_rows):
    h_ref, ps_ref = ins
    wu, wz = weights
    o_ref, = outs
    exts = scratch
    tm, cb = o_ref.shape

    @pl.when(i % tiles_per_seq == 0)
    def _():
        for e in exts:
            e[0:POOL_HALO, :] = jnp.zeros((POOL_HALO, cb), F32)

    win = jnp.where(g == 0, POOL_WINDOWS[0],
                    jnp.where(g == 1, POOL_WINDOWS[1],
                              jnp.where(g == 2, POOL_WINDOWS[2], POOL_WINDOWS[3]))).astype(F32)
    w_u, w_z = wu(), wz()
    for lo in range(0, tm, sub_rows):
        h = h_ref[lo:lo + sub_rows, :]
        u = _dot(h, w_u)
        z = _dot(h, w_z)
        sums = []
        s = u
        top = POOL_HALO + lo
        for e, w in zip(exts, (1, 2, 4, 8)):
            e[top:top + sub_rows, :] = s
            s = s + e[top - w:top - w + sub_rows, :]
            sums.append(s)
        wsum = jnp.where(g == 0, sums[0], jnp.where(g == 1, sums[1], jnp.where(g == 2, sums[2], sums[3])))
        pos = ((i % tiles_per_seq) * tm + lo
               + lax.broadcasted_iota(jnp.int32, (sub_rows, 1), 0)).astype(F32)
        cnt = jnp.minimum(pos + 1.0, win)
        o_ref[lo:lo + sub_rows, :] = ((wsum / cnt - u) * ps_ref[...] * _silu(z)).astype(o_ref.dtype)
    for e in exts:
        e[0:POOL_HALO, :] = e[tm:tm + POOL_HALO, :]


def _conf_compute(c, i, ins, weights, outs, scratch, *, tiles_per_seq, rows_per_chunk, sub_rows):
    h_ref, cw_ref, cbias_ref = ins
    wa, wg, wz = weights
    d_ref, zc_ref = outs
    ext_ref, wrow_ref, brow_ref, odd_ref = scratch
    tm, cb = d_ref.shape
    kw = cw_ref.shape[0]
    grp = BF16_ROWS
    n_grp = rows_per_chunk // grp

    @pl.when(i == 0)
    def _():
        for k in range(kw):
            wrow_ref[k] = jnp.broadcast_to(cw_ref[k:k + 1, :], (grp, cb)).astype(wrow_ref.dtype)
        brow_ref[...] = jnp.broadcast_to(cbias_ref[...], (grp, cb))

    @pl.when(i % tiles_per_seq == 0)
    def _():
        for e in range(ext_ref.shape[0]):
            ext_ref[e] = jnp.zeros(ext_ref.shape[1:], ext_ref.dtype)
        odd_ref[0:grp, :] = jnp.zeros((grp, cb), F32)

    w_a, w_g, w_z = wa(), wg(), wz()
    for lo in range(0, tm, sub_rows):
        h = h_ref[lo:lo + sub_rows, :]
        z = _dot(h, w_z)
        zc_ref[lo:lo + sub_rows, :] = _silu(z).astype(zc_ref.dtype)
        glu = (_dot(h, w_a) * jax.nn.sigmoid(_dot(h, w_g))).astype(ext_ref.dtype)
        for e in range(ext_ref.shape[0]):
            top = CONV_HALO + 2 * e + lo
            ext_ref[e, top:top + sub_rows, :] = glu

    lane_tile = min(cb, CONV_LANE_TILE)

    def conv_chunk(r, carry):
        base = pl.multiple_of(r * rows_per_chunk, rows_per_chunk)
        for lane0 in range(0, cb, lane_tile):
            lanes = slice(lane0, lane0 + lane_tile)
            sums = []
            for parity in (0, 1):
                accs = [None] * n_grp
                for k in range(kw):
                    blocks, s = divmod(kw - 1 - k, grp)
                    if s % 2 != parity:
                        continue
                    for j in range(n_grp):
                        row = base + (CONV_HALO + grp * (j - blocks))
                        term = (ext_ref[s // 2, pl.ds(row, grp), lanes].astype(F32)
                                * wrow_ref[k, :, lanes].astype(F32))
                        accs[j] = term if accs[j] is None else accs[j] + term
                sums.append(accs)
            even, odd = sums
            for j in range(n_grp):
                odd_ref[grp * (j + 1):grp * (j + 2), lanes] = odd[j]
            for j in range(n_grp):
                d_ref[pl.ds(base + grp * j, grp), lanes] = (
                    even[j] + brow_ref[:, lanes]
                    + odd_ref[grp * (j + 1) - 1:grp * (j + 2) - 1, lanes])
            odd_ref[0:grp, lanes] = odd[n_grp - 1]
        return carry

    lax.fori_loop(0, tm // rows_per_chunk, conv_chunk, 0)
    keep = CONV_HALO + grp
    for e in range(ext_ref.shape[0]):
        ext_ref[e, 0:keep, :] = ext_ref[e, tm:tm + keep, :]


def _attn_compute(c, i, ins, weights, outs, scratch):
    h_ref, k_ref, v_ref = ins
    o_ref, = outs
    q = _dot(h_ref[...], weights[0]())
    dh = q.shape[1] // X_HEADS
    scale = dh ** -0.5
    for hd in range(X_HEADS):
        cols = slice(hd * dh, (hd + 1) * dh)
        qh = q[:, cols].astype(BF16)
        s = lax.dot_general(qh, k_ref[0, :, cols], (((1,), (1,)), ((), ())),
                            preferred_element_type=F32) * scale
        p = jnp.exp(s - jnp.max(s, axis=-1, keepdims=True))
        p = p / jnp.sum(p, axis=-1, keepdims=True)
        o_ref[:, cols] = _dot(p.astype(BF16), v_ref[0, :, cols]).astype(o_ref.dtype)


def _gates_compute(c, i, ins, weights, outs, scratch, *, sub_rows):
    h_ref, = ins
    o_ref, = outs
    w = weights[0]()
    for lo in range(0, o_ref.shape[0], sub_rows):
        o_ref[lo:lo + sub_rows, :] = jax.nn.sigmoid(_dot(h_ref[lo:lo + sub_rows, :], w)).astype(o_ref.dtype)


def _merge_compute(c, i, ins, weights, outs, scratch):
    ya_ref, pb_ref, sc_ref, att_ref, g0_ref, g1_ref, g2_ref, g3_ref = ins
    woa, wp, woc, wox = weights
    m = g0_ref[...].astype(F32) * _dot(ya_ref[...], woa())
    m = m + g1_ref[...].astype(F32) * _dot(pb_ref[...], wp())
    m = m + g2_ref[...].astype(F32) * _dot(sc_ref[...], woc())
    m = m + g3_ref[...].astype(F32) * _dot(att_ref[...], wox())
    outs[0][...] = m.astype(outs[0].dtype)


def _matmul_compute(c, i, ins, weights, outs, scratch):
    outs[0][...] = _dot(ins[0][...], weights[0]()).astype(outs[0].dtype)


def _layer(layer, h, x, kv, seq_len, w_in, conv_a_w, w_out_a, pool_scale, w_pool, conv_c_w,
           conv_c_b, ln_c_g, ln_c_b, w_out_c, w_out_x, w_o, g_post, g_next):
    t, d = h.shape
    e_a = conv_a_w.shape[1]
    e_b = pool_scale.shape[0]
    e_c = conv_c_w.shape[1]
    x_w = w_out_x.shape[1]
    n_groups, g_b, go_b = w_pool.shape[1:]
    assert n_groups == len(POOL_WINDOWS) and n_groups * go_b == d
    off_va, off_ba, off_ca, off_za = 0, e_a, 2 * e_a, 3 * e_a
    off_ub = 4 * e_a
    off_zb = off_ub + e_b
    off_ac = off_zb + e_b
    off_gc = off_ac + e_c
    off_zc = off_gc + e_c
    off_q = off_zc + e_c
    off_g = off_q + x_w
    assert off_g + 4 * d == w_in.shape[2]

    def in_part(col0, cb):
        return _WPart(w_in, _cols_of(layer, col0, cb), d, cb)

    tm_conv = _row_tile(seq_len, CONV_ROW_TILE)
    tps_conv = seq_len // tm_conv
    ni_conv = t // tm_conv
    cb_a = min(e_a, 512)
    cb_c = min(e_c, 512)
    h_row = pl.BlockSpec((tm_conv, d), lambda c, i: (i, 0))

    ka = conv_a_w.shape[0]
    ya, = _ws_call(
        functools.partial(_conv_a_compute, tiles_per_seq=tps_conv, sub_rows=min(SUB_ROWS_A, tm_conv)),
        name="branch_a", n_c=e_a // cb_a, n_i=ni_conv,
        parts=[in_part(o, cb_a) for o in (off_va, off_ba, off_ca, off_za)],
        blocked=[h, conv_a_w],
        blocked_specs=[h_row, pl.BlockSpec((ka, cb_a), lambda c, i: (0, c))],
        out_shape=[jax.ShapeDtypeStruct((t, e_a), BF16)],
        out_specs=[pl.BlockSpec((tm_conv, cb_a), lambda c, i: (i, c))],
        scratch=[pltpu.VMEM((CONV_HALO + tm_conv, cb_a), F32)],
        est_bytes=2 * tm_conv * d * 2 + 14 * tm_conv * cb_a * 4)

    tm_b = _row_tile(seq_len, POOL_ROW_TILE)
    pb, = _ws_call(
        functools.partial(_pool_compute, tiles_per_seq=seq_len // tm_b, sub_rows=min(SUB_ROWS_B, tm_b)),
        name="branch_b", n_c=n_groups, n_i=t // tm_b,
        parts=[in_part(o, g_b) for o in (off_ub, off_zb)],
        blocked=[h, pool_scale.reshape(1, e_b)],
        blocked_specs=[pl.BlockSpec((tm_b, d), lambda c, i: (i, 0)),
                       pl.BlockSpec((1, g_b), lambda c, i: (0, c))],
        out_shape=[jax.ShapeDtypeStruct((t, e_b), BF16)],
        out_specs=[pl.BlockSpec((tm_b, g_b), lambda c, i: (i, c))],
        scratch=[pltpu.VMEM((POOL_HALO + tm_b, g_b), F32)] * 4,
        est_bytes=2 * tm_b * d * 2 + 14 * tm_b * g_b * 4)

    kc = conv_c_w.shape[0]
    assert (kc - 1) // BF16_ROWS * BF16_ROWS <= CONV_HALO and off_q % x_w == 0 and off_g % go_b == 0
    sub_c = min(SUB_ROWS_C, tm_conv)
    chunk_c = min(CONV_CHUNK_ROWS, tm_conv)
    dc, zc = _ws_call(
        functools.partial(_conf_compute, tiles_per_seq=tps_conv, rows_per_chunk=chunk_c, sub_rows=sub_c),
        name="branch_c", n_c=e_c // cb_c, n_i=ni_conv,
        parts=[in_part(o, cb_c) for o in (off_ac, off_gc, off_zc)],
        blocked=[h, conv_c_w, conv_c_b.reshape(1, e_c)],
        blocked_specs=[h_row, pl.BlockSpec((kc, cb_c), lambda c, i: (0, c)),
                       pl.BlockSpec((1, cb_c), lambda c, i: (0, c))],
        out_shape=[jax.ShapeDtypeStruct((t, e_c), F32), jax.ShapeDtypeStruct((t, e_c), BF16)],
        out_specs=[pl.BlockSpec((tm_conv, cb_c), lambda c, i: (i, c))] * 2,
        scratch=[pltpu.VMEM((BF16_ROWS // 2, CONV_HALO + tm_conv + BF16_ROWS, cb_c), BF16),
                 pltpu.VMEM((kc, BF16_ROWS, cb_c), BF16),
                 pltpu.VMEM((BF16_ROWS, cb_c), F32),
                 pltpu.VMEM((BF16_ROWS + chunk_c, cb_c), F32)],
        est_bytes=2 * tm_conv * d * 2 + 22 * tm_conv * cb_c * 4)
    sc = _ln_gate(dc, zc, ln_c_g, ln_c_b)

    tm_x = _row_tile(seq_len, CONV_ROW_TILE)
    tps_x = seq_len // tm_x
    n_mem = kv.shape[1]
    att, = _ws_call(
        _attn_compute, name="mem_attention", n_c=1, n_i=t // tm_x,
        parts=[in_part(off_q, x_w)],
        blocked=[h, kv, kv],
        blocked_specs=[pl.BlockSpec((tm_x, d), lambda c, i: (i, 0)),
                       pl.BlockSpec((1, n_mem, x_w), lambda c, i: (i // tps_x, 0, 0)),
                       pl.BlockSpec((1, n_mem, x_w), lambda c, i: (i // tps_x, 0, 1))],
        out_shape=[jax.ShapeDtypeStruct((t, x_w), BF16)],
        out_specs=[pl.BlockSpec((tm_x, x_w), lambda c, i: (i, 0))],
        est_bytes=2 * tm_x * d * 2 + 10 * tm_x * x_w * 4)

    tm_g = _row_tile(t, MATMUL_ROW_TILE)
    nb = go_b
    gates, = _ws_call(
        functools.partial(_gates_compute, sub_rows=min(SUB_ROWS_GATES, tm_g)),
        name="merge_gates", n_c=4 * d // nb, n_i=t // tm_g,
        parts=[in_part(off_g, nb)],
        blocked=[h],
        blocked_specs=[pl.BlockSpec((tm_g, d), lambda c, i: (i, 0))],
        out_shape=[jax.ShapeDtypeStruct((t, 4 * d), BF16)],
        out_specs=[pl.BlockSpec((tm_g, nb), lambda c, i: (i, c))],
        est_bytes=2 * (tm_g * d * 2 + tm_g * nb * 2) + 3 * tm_g * nb * 4)

    tm_m = _row_tile(t, CONV_ROW_TILE)
    n_nb = d // nb
    row = lambda width: pl.BlockSpec((tm_m, width), lambda n, i: (i, 0))
    gate = lambda k: pl.BlockSpec((tm_m, nb), lambda n, i: (i, k * n_nb + n))
    merged, = _ws_call(
        _merge_compute, name="gated_merge", n_c=n_nb, n_i=t // tm_m,
        parts=[_WPart(w_out_a, _cols_of(layer, 0, nb), e_a, nb),
               _WPart(w_pool, lambda ref, c, row0, rows: ref.at[layer, c, pl.ds(row0, rows), :],
                      g_b, go_b),
               _WPart(w_out_c, _cols_of(layer, 0, nb), e_c, nb),
               _WPart(w_out_x, _cols_of(layer, 0, nb), x_w, nb)],
        blocked=[ya, pb, sc, att, gates, gates, gates, gates],
        blocked_specs=[row(e_a), pl.BlockSpec((tm_m, g_b), lambda n, i: (i, n)), row(e_c), row(x_w),
                       gate(0), gate(1), gate(2), gate(3)],
        out_shape=[jax.ShapeDtypeStruct((t, d), BF16)],
        out_specs=[pl.BlockSpec((tm_m, nb), lambda n, i: (i, n))],
        est_bytes=2 * (tm_m * (e_a + g_b + e_c + x_w) * 2 + 5 * tm_m * nb * 2) + 6 * tm_m * nb * 4)

    tm_o = _row_tile(t, MATMUL_ROW_TILE)
    y, = _ws_call(
        _matmul_compute, name="out_proj", n_c=d // nb, n_i=t // tm_o,
        parts=[_WPart(w_o, _cols_of(layer, 0, nb), d, nb)],
        blocked=[merged],
        blocked_specs=[pl.BlockSpec((tm_o, d), lambda n, i: (i, 0))],
        out_shape=[jax.ShapeDtypeStruct((t, d), F32)],
        out_specs=[pl.BlockSpec((tm_o, nb), lambda n, i: (i, n))],
        est_bytes=2 * (tm_o * d * 2 + tm_o * nb * 4) + 2 * tm_o * nb * 4)
    return _residual(x, y, g_post, g_next)


def kernel(x, mem, g_pre, g_post, g_mem, w_in, conv_a_w, w_out_a, pool_scale, w_pool, conv_c_w,
           conv_c_b, ln_c_g, ln_c_b, w_out_c, w_mem_kv, w_out_x, w_o):
    bsz, seq_len, d = x.shape
    depth = w_in.shape[0]
    xf = x.reshape(bsz * seq_len, d)
    h = _rmsnorm(xf, g_pre[0], BF16)
    for l in range(depth):
        kv = _memory_kv(mem, g_mem[l], w_mem_kv, l)
        g_next = g_pre[l + 1] if l + 1 < depth else None
        xf, h = _layer(
            l, h, xf, kv, seq_len, w_in, conv_a_w[l], w_out_a, pool_scale[l], w_pool, conv_c_w[l],
            conv_c_b[l], ln_c_g[l], ln_c_b[l], w_out_c, w_out_x, w_o, g_post[l], g_next)
    return xf.reshape(bsz, seq_len, d)
```
